```python
import functools
import jax, jax.numpy as jnp
from jax import lax
import numpy as np

D_MODEL = 2048
BATCH = 2
SEQ = 4096
DEPTH = 1
DEC_BATCH = 32
DEC_SEQ = 4
PAST_LEN = 16384
PAGE_SIZE = 128

HEAD_DIM = 128
A_HEADS = 8
A_KV_HEADS = 4
B_HEADS = 8
B_KV_HEADS = 4
IDX_HEADS = 16
IDX_DIM = 64
ROPE_THETA = 500000.0
ROPE_FRAC = 4
Q_BLOCK = 128
TOPK_MAX = 256
PEER_KEYS = 128
PEER_EXPERTS = PEER_KEYS * PEER_KEYS
PEER_HEADS = 8
PEER_KEY_DIM = 256
PEER_TOPK = 16
PEER_BLOCK = 128
FORGET_BIAS = 5.0
EPS = 1e-6
ATTN_SCALE = HEAD_DIM ** -0.5
N_ADA = 6
IN_SIZES = (A_HEADS * HEAD_DIM, A_KV_HEADS * HEAD_DIM, A_KV_HEADS * HEAD_DIM, A_HEADS,
            B_HEADS * HEAD_DIM, B_KV_HEADS * HEAD_DIM, B_KV_HEADS * HEAD_DIM,
            IDX_HEADS * IDX_DIM, IDX_DIM, IDX_HEADS, D_MODEL, D_MODEL)
IN_WIDTH = sum(IN_SIZES)

kernel_name = 'fox_dsa_peer_hybrid_step'


def _rmsnorm(x, g):
    xf = x.astype(jnp.float32)
    y = xf * lax.rsqrt(jnp.mean(xf * xf, axis=-1, keepdims=True) + EPS)
    return (y * g.astype(jnp.float32)).astype(x.dtype)


def _rope_partial(x, pos):
    rd = x.shape[-1] // ROPE_FRAC
    half = rd // 2
    inv = ROPE_THETA ** (-jnp.arange(half, dtype=jnp.float32) / half)
    ang = pos.astype(jnp.float32)[:, None] * inv[None, :]
    cos = jnp.cos(ang)[:, None, :]
    sin = jnp.sin(ang)[:, None, :]
    xf = x.astype(jnp.float32)
    x1, x2, rest = xf[..., :half], xf[..., half:rd], xf[..., rd:]
    out = jnp.concatenate([x1 * cos - x2 * sin, x2 * cos + x1 * sin, rest], axis=-1)
    return out.astype(x.dtype)


def _in_project(h, pos, w_in, b_fgate, g_qn_a, g_kn_a, g_qn_b, g_kn_b):
    B, T, _ = h.shape
    z = jnp.einsum('btd,de->bte', h, w_in)
    splits = np.cumsum(IN_SIZES)[:-1].tolist()
    q_a, k_a, v_a, f_a, q_b, k_b, v_b, q_i, k_i, w_i, gate_a, gate_b = jnp.split(z, splits, axis=-1)
    q_a = _rmsnorm(q_a.reshape(B, T, A_HEADS, HEAD_DIM), g_qn_a)
    k_a = _rmsnorm(k_a.reshape(B, T, A_KV_HEADS, HEAD_DIM), g_kn_a)
    v_a = v_a.reshape(B, T, A_KV_HEADS, HEAD_DIM)
    logf = jax.nn.log_sigmoid(f_a.astype(jnp.float32) + b_fgate.astype(jnp.float32))
    q_b = _rope_partial(_rmsnorm(q_b.reshape(B, T, B_HEADS, HEAD_DIM), g_qn_b), pos)
    k_b = _rope_partial(_rmsnorm(k_b.reshape(B, T, B_KV_HEADS, HEAD_DIM), g_kn_b), pos)
    v_b = v_b.reshape(B, T, B_KV_HEADS, HEAD_DIM)
    q_i = _rope_partial(q_i.reshape(B, T, IDX_HEADS, IDX_DIM), pos)
    k_i = _rope_partial(k_i.reshape(B, T, 1, IDX_DIM), pos)[:, :, 0]
    return (q_a, k_a, v_a, logf, q_b, k_b, v_b, q_i, k_i, w_i, gate_a, gate_b)


def _gather_pages(cache, l, page_table):
    g = cache[l, page_table]
    return g.reshape(page_table.shape[0], page_table.shape[1] * PAGE_SIZE, *cache.shape[3:])


def _fox_prompt(q, k, v, logf):
    B, T, H, Dh = q.shape
    G = H // A_KV_HEADS
    nblk = T // Q_BLOCK
    F = jnp.cumsum(logf, axis=1)
    Fk = F.reshape(B, T, A_KV_HEADS, G).transpose(0, 2, 3, 1)
    qb = jnp.moveaxis(q.reshape(B, nblk, Q_BLOCK, A_KV_HEADS, G, Dh), 1, 0)
    Fqb = jnp.moveaxis(Fk.reshape(B, A_KV_HEADS, G, nblk, Q_BLOCK), 3, 0)
    key_pos = jnp.arange(T)

    def block(args):
        qi, Fqi, i = args
        s = jnp.einsum('bqkgd,bskd->bkgqs', qi, k).astype(jnp.float32) * ATTN_SCALE
        s = s + Fqi[..., None] - Fk[..., None, :]
        qpos = i * Q_BLOCK + jnp.arange(Q_BLOCK)
        s = jnp.where(key_pos[None, :] <= qpos[:, None], s, -jnp.inf)
        p = jax.nn.softmax(s, axis=-1).astype(v.dtype)
        return jnp.einsum('bkgqs,bskd->bqkgd', p, v)

    o = lax.map(block, (qb, Fqb, jnp.arange(nblk)))
    return jnp.moveaxis(o, 0, 1).reshape(B, T, H * Dh)


def _fox_sample(q, k, v, logf, cache_k, cache_v, cache_logf, l, page_table):
    B, T, H, Dh = q.shape
    G = H // A_KV_HEADS
    pk = _gather_pages(cache_k, l, page_table)
    pv = _gather_pages(cache_v, l, page_table)
    plf = _gather_pages(cache_logf, l, page_table).astype(jnp.float32)
    P = pk.shape[1]
    F_past = jnp.cumsum(plf, axis=1)
    F_new = F_past[:, -1:] + jnp.cumsum(logf, axis=1)
    Fq = F_new.reshape(B, T, A_KV_HEADS, G).transpose(0, 2, 3, 1)
    Fp = F_past.reshape(B, P, A_KV_HEADS, G).transpose(0, 2, 3, 1)
    qg = q.reshape(B, T, A_KV_HEADS, G, Dh)
    s_past = jnp.einsum('btkgd,bpkd->bkgtp', qg, pk).astype(jnp.float32) * ATTN_SCALE + Fq[..., None] - Fp[..., None, :]
    s_new = jnp.einsum('btkgd,bskd->bkgts', qg, k).astype(jnp.float32) * ATTN_SCALE + Fq[..., None] - Fq[..., None, :]
    causal = jnp.arange(T)[None, :] <= jnp.arange(T)[:, None]
    s_new = jnp.where(causal, s_new, -jnp.inf)
    p = jax.nn.softmax(jnp.concatenate([s_past, s_new], axis=-1), axis=-1).astype(v.dtype)
    o = jnp.einsum('bkgtp,bpkd->btkgd', p[..., :P], pv) + jnp.einsum('bkgts,bskd->btkgd', p[..., P:], v)
    return o.reshape(B, T, H * Dh)


def _indexer_scores(q_i, w_i, k_i):
    dots = jax.nn.relu(jnp.einsum('bthd,bsd->bths', q_i, k_i).astype(jnp.float32))
    return jnp.einsum('bth,bths->bts', w_i.astype(jnp.float32), dots)


def _sparse_attend(q, k_sel, v_sel, valid):
    B, T, H, Dh = q.shape
    G = H // B_KV_HEADS
    qg = q.reshape(B, T, B_KV_HEADS, G, Dh)
    s = jnp.einsum('btkgd,btnkd->bkgtn', qg, k_sel).astype(jnp.float32) * ATTN_SCALE
    s = jnp.where(valid[:, None, None], s, -jnp.inf)
    p = jax.nn.softmax(s, axis=-1).astype(v_sel.dtype)
    o = jnp.einsum('bkgtn,btnkd->btkgd', p, v_sel)
    return o.reshape(B, T, H * Dh)


def _dsa_prompt(q, k, v, q_i, k_i, w_i):
    B, T, H, Dh = q.shape
    nblk = T // Q_BLOCK
    topk = min(TOPK_MAX, T // 4)
    key_pos = jnp.arange(T)
    bidx = jnp.arange(B)[:, None, None]
    qb = jnp.moveaxis(q.reshape(B, nblk, Q_BLOCK, H, Dh), 1, 0)
    qib = jnp.moveaxis(q_i.reshape(B, nblk, Q_BLOCK, IDX_HEADS, IDX_DIM), 1, 0)
    wib = jnp.moveaxis(w_i.reshape(B, nblk, Q_BLOCK, IDX_HEADS), 1, 0)

    def block(args):
        qq, qi, wi, i = args
        I = _indexer_scores(qi, wi, k_i)
        qpos = i * Q_BLOCK + jnp.arange(Q_BLOCK)
        I = jnp.where(key_pos[None, None, :] <= qpos[None, :, None], I, -jnp.inf)
        vals, sel = lax.top_k(I, topk)
        valid = jnp.isfinite(vals)
        return _sparse_attend(qq, k[bidx, sel], v[bidx, sel], valid)

    o = lax.map(block, (qb, qib, wib, jnp.arange(nblk)))
    return jnp.moveaxis(o, 0, 1).reshape(B, T, H * Dh)


def _dsa_sample(q, k, v, q_i, k_i, w_i, cache_k, cache_v, cache_idx_k, l, page_table):
    B, T = q.shape[:2]
    pki = _gather_pages(cache_idx_k, l, page_table)
    P = pki.shape[1]
    topk = min(TOPK_MAX, (P + T) // 4)
    I_past = _indexer_scores(q_i, w_i, pki)
    causal = jnp.arange(T)[None, :] <= jnp.arange(T)[:, None]
    I_new = jnp.where(causal[None], _indexer_scores(q_i, w_i, k_i), -jnp.inf)
    vals, sel = lax.top_k(jnp.concatenate([I_past, I_new], axis=-1), topk)
    valid = jnp.isfinite(vals)
    bidx = jnp.arange(B)[:, None, None]
    ps = jnp.minimum(sel, P - 1)
    phys = page_table[bidx, ps // PAGE_SIZE]
    row = ps % PAGE_SIZE
    ns = jnp.clip(sel - P, 0, T - 1)
    from_past = (sel < P)[..., None, None]
    k_sel = jnp.where(from_past, cache_k[l, phys, row], k[bidx, ns])
    v_sel = jnp.where(from_past, cache_v[l, phys, row], v[bidx, ns])
    return _sparse_attend(q, k_sel, v_sel, valid)


def _peer_tokens(h, w_q, subkeys, u, v):
    N = h.shape[0]
    q = jnp.einsum('nd,de->ne', h, w_q).reshape(N, PEER_HEADS, 2, PEER_KEY_DIM // 2)
    s = jnp.einsum('nhcd,hckd->nhck', q, subkeys).astype(jnp.float32)
    sv, si = lax.top_k(s, PEER_TOPK)
    cand = (sv[:, :, 0, :, None] + sv[:, :, 1, None, :]).reshape(N, PEER_HEADS, PEER_TOPK * PEER_TOPK)
    cid = (si[:, :, 0, :, None] * PEER_KEYS + si[:, :, 1, None, :]).reshape(N, PEER_HEADS, PEER_TOPK * PEER_TOPK)
    top_s, ci = lax.top_k(cand, PEER_TOPK)
    eid = jnp.take_along_axis(cid, ci, axis=-1)
    g = jax.nn.softmax(top_s, axis=-1)
    a = jax.nn.gelu(jnp.einsum('nd,nhed->nhe', h, u[eid]).astype(jnp.float32), approximate=False)
    return jnp.einsum('nhe,nhed->nd', (g * a).astype(h.dtype), v[eid])


def _peer_ffn(h, w_q, subkeys, u, v):
    B, T, D = h.shape
    n = B * T
    flat = h.reshape(n, D)
    if n % PEER_BLOCK == 0 and n > PEER_BLOCK:
        blocks = flat.reshape(n // PEER_BLOCK, PEER_BLOCK, D)
        out = lax.map(lambda hb: _peer_tokens(hb, w_q, subkeys, u, v), blocks).reshape(n, D)
    else:
        out = _peer_tokens(flat, w_q, subkeys, u, v)
    return out.reshape(B, T, D)


def _prompt_mixer(q_a, k_a, v_a, logf, q_b, k_b, v_b, q_i, k_i, w_i):
    return (_fox_prompt(q_a, k_a, v_a, logf), _dsa_prompt(q_b, k_b, v_b, q_i, k_i, w_i))


def _sample_mixer(caches, l, page_table, q_a, k_a, v_a, logf, q_b, k_b, v_b, q_i, k_i, w_i):
    ck_a, cv_a, clf, ck_b, cv_b, cki = caches
    o_a = _fox_sample(q_a, k_a, v_a, logf, ck_a, cv_a, clf, l, page_table)
    o_b = _dsa_sample(q_b, k_b, v_b, q_i, k_i, w_i, ck_b, cv_b, cki, l, page_table)
    return (o_a, o_b)


def _run_group(x, c, pos, mixer, p):
    (w_ada, b_ada, g_norm1, g_norm2, w_in, b_fgate, g_qn_a, g_kn_a, g_qn_b, g_kn_b,
     w_branch_a, w_branch_b, w_out, w_peer_q, peer_subkeys, peer_u, peer_v) = p
    B = x.shape[0]
    ada = (jnp.einsum('bd,de->be', jax.nn.silu(c), w_ada) + b_ada).reshape(B, N_ADA, 1, D_MODEL)
    sh1, sc1, ga1, sh2, sc2, ga2 = [ada[:, i] for i in range(N_ADA)]
    h = _rmsnorm(x, g_norm1) * (1 + sc1) + sh1
    q_a, k_a, v_a, logf, q_b, k_b, v_b, q_i, k_i, w_i, gate_a, gate_b = _in_project(
        h, pos, w_in, b_fgate, g_qn_a, g_kn_a, g_qn_b, g_kn_b)
    o_a, o_b = mixer(q_a, k_a, v_a, logf, q_b, k_b, v_b, q_i, k_i, w_i)
    merged = (jax.nn.sigmoid(gate_a) * jnp.einsum('bte,ed->btd', o_a, w_branch_a)
              + jax.nn.sigmoid(gate_b) * jnp.einsum('bte,ed->btd', o_b, w_branch_b))
    x = x + ga1 * jnp.einsum('btd,de->bte', merged, w_out)
    h2 = _rmsnorm(x, g_norm2) * (1 + sc2) + sh2
    x = x + ga2 * _peer_ffn(h2, w_peer_q, peer_subkeys, peer_u, peer_v)
    return x, (k_a, v_a, logf, k_b, v_b, k_i)


def setup_inputs(seed: int = 0) -> dict:
    key = jax.random.key(seed)
    ks = iter(jax.random.split(key, 40))
    n_pages = PAST_LEN // PAGE_SIZE
    used = DEC_BATCH * n_pages
    n_pool = used + max(1, used // 4)
    D = D_MODEL

    def nrm(shape, s):
        return jax.random.normal(next(ks), shape, jnp.float32) * s

    inp = {}
    inp['x_prompt'] = nrm((BATCH, SEQ, D), 1.0)
    inp['x_sample'] = nrm((DEC_BATCH, DEC_SEQ, D), 1.0)
    inp['cache_fox_k'] = nrm((DEPTH, n_pool, PAGE_SIZE, A_KV_HEADS, HEAD_DIM), 1.0)
    inp['cache_fox_v'] = nrm((DEPTH, n_pool, PAGE_SIZE, A_KV_HEADS, HEAD_DIM), 1.0)
    inp['cache_fox_logf'] = jax.nn.log_sigmoid(FORGET_BIAS + nrm((DEPTH, n_pool, PAGE_SIZE, A_HEADS), 1.0))
    inp['cache_dsa_k'] = nrm((DEPTH, n_pool, PAGE_SIZE, B_KV_HEADS, HEAD_DIM), 1.0)
    inp['cache_dsa_v'] = nrm((DEPTH, n_pool, PAGE_SIZE, B_KV_HEADS, HEAD_DIM), 1.0)
    inp['cache_idx_k'] = nrm((DEPTH, n_pool, PAGE_SIZE, IDX_DIM), 1.0)
    inp['page_table'] = jax.random.permutation(next(ks), n_pool)[:used].reshape(DEC_BATCH, n_pages).astype(jnp.int32)
    inp['c_prompt'] = nrm((BATCH, D), 1.0)
    inp['c_sample'] = nrm((DEC_BATCH, D), 1.0)
    inp['w_ada'] = nrm((DEPTH, D, N_ADA * D), 0.5 * D ** -0.5)
    inp['b_ada'] = nrm((DEPTH, N_ADA * D), 0.02)
    inp['g_norm1'] = 1.0 + nrm((DEPTH, D), 0.02)
    inp['g_norm2'] = 1.0 + nrm((DEPTH, D), 0.02)
    inp['w_in'] = nrm((DEPTH, D, IN_WIDTH), D ** -0.5)
    inp['b_fgate'] = FORGET_BIAS + nrm((DEPTH, A_HEADS), 0.1)
    inp['g_qn_a'] = 1.0 + nrm((DEPTH, HEAD_DIM), 0.02)
    inp['g_kn_a'] = 1.0 + nrm((DEPTH, HEAD_DIM), 0.02)
    inp['g_qn_b'] = 1.0 + nrm((DEPTH, HEAD_DIM), 0.02)
    inp['g_kn_b'] = 1.0 + nrm((DEPTH, HEAD_DIM), 0.02)
    inp['w_branch_a'] = nrm((DEPTH, A_HEADS * HEAD_DIM, D), (A_HEADS * HEAD_DIM) ** -0.5)
    inp['w_branch_b'] = nrm((DEPTH, B_HEADS * HEAD_DIM, D), (B_HEADS * HEAD_DIM) ** -0.5)
    inp['w_out'] = nrm((DEPTH, D, D), D ** -0.5)
    inp['w_peer_q'] = nrm((DEPTH, D, PEER_HEADS * PEER_KEY_DIM), D ** -0.5)
    inp['peer_subkeys'] = nrm((DEPTH, PEER_HEADS, 2, PEER_KEYS, PEER_KEY_DIM // 2), (PEER_KEY_DIM // 2) ** -0.5)
    inp['peer_u'] = nrm((DEPTH, PEER_EXPERTS, D), D ** -0.5)
    inp['peer_v'] = nrm((DEPTH, PEER_EXPERTS, D), 0.5)
    return inp


def reference(x_prompt, x_sample, cache_fox_k, cache_fox_v, cache_fox_logf, cache_dsa_k, cache_dsa_v,
              cache_idx_k, page_table, c_prompt, c_sample, w_ada, b_ada, g_norm1, g_norm2, w_in, b_fgate,
              g_qn_a, g_kn_a, g_qn_b, g_kn_b, w_branch_a, w_branch_b, w_out, w_peer_q, peer_subkeys,
              peer_u, peer_v):
    past_len = page_table.shape[1] * PAGE_SIZE
    pos_prompt = jnp.arange(x_prompt.shape[1])
    pos_sample = past_len + jnp.arange(x_sample.shape[1])
    caches = (cache_fox_k, cache_fox_v, cache_fox_logf, cache_dsa_k, cache_dsa_v, cache_idx_k)
    xp, xs = x_prompt, x_sample
    rows_p, rows_s = [], []
    for l in range(DEPTH):
        p = (w_ada[l], b_ada[l], g_norm1[l], g_norm2[l], w_in[l], b_fgate[l], g_qn_a[l], g_kn_a[l],
             g_qn_b[l], g_kn_b[l], w_branch_a[l], w_branch_b[l], w_out[l], w_peer_q[l],
             peer_subkeys[l], peer_u[l], peer_v[l])
        xp, r_p = _run_group(xp, c_prompt, pos_prompt, _prompt_mixer, p)
        xs, r_s = _run_group(xs, c_sample, pos_sample, functools.partial(_sample_mixer, caches, l, page_table), p)
        rows_p.append(r_p)
        rows_s.append(r_s)
    fk_p, fv_p, flf_p, dk_p, dv_p, ik_p = [jnp.stack([r[i] for r in rows_p], axis=0) for i in range(6)]
    fk_s, fv_s, flf_s, dk_s, dv_s, ik_s = [jnp.stack([r[i] for r in rows_s], axis=0) for i in range(6)]
    return (xp, xs, fk_p, fv_p, flf_p, dk_p, dv_p, ik_p, fk_s, fv_s, flf_s, dk_s, dv_s, ik_s)
```

```python
import functools

import jax
import jax.numpy as jnp
from jax import lax
from jax.experimental import pallas as pl
from jax.experimental.pallas import tpu as pltpu

F32 = jnp.float32
BF16 = jnp.bfloat16
I32 = jnp.int32

HEAD_DIM = 128
A_HEADS = 8
A_KV_HEADS = 4
B_HEADS = 8
B_KV_HEADS = 4
IDX_HEADS = 16
IDX_DIM = 64
ROPE_THETA = 500000.0
ROPE_FRAC = 4
TOPK_MAX = 256
PEER_KEYS = 128
PEER_HEADS = 8
PEER_TOPK = 16
PAGE_SIZE = 128
EPS = 1e-6
ATTN_SCALE = HEAD_DIM ** -0.5
N_ADA = 6
LANES = 128
NEG = -1e30
INT_MIN = -(2 ** 31)
MASKED_KEY = -2139095041
VMEM_LIMIT = 56 * 1024 * 1024


def _cparams(*sem):
    return pltpu.CompilerParams(dimension_semantics=sem, vmem_limit_bytes=VMEM_LIMIT)


def _tile(n, pref, mult=LANES):
    best = None
    t = mult
    while t <= min(n, pref):
        if n % t == 0:
            best = t
        t += mult
    return best if best is not None else n


def _dot(a, b):
    return jnp.dot(a, b, preferred_element_type=F32)


def _dot_nt(a, b):
    return lax.dot_general(a, b, (((1,), (1,)), ((), ())), preferred_element_type=F32)


def _sigmoid(x):
    return 1.0 / (1.0 + jnp.exp(-x))


def _float_key(x):
    b = pltpu.bitcast(x, I32)
    return b ^ ((b >> 31) & jnp.int32(0x7FFFFFFF))


def _ada_kernel(c_ref, w_ref, b_ref, o_ref):
    c = c_ref[...]
    s = c * _sigmoid(c)
    o_ref[...] = _dot(s.astype(BF16), w_ref[...].astype(BF16)) + b_ref[...]


def _ada(c_all, w_ada, b_ada):
    m, d = c_all.shape
    n = w_ada.shape[1]
    tn = _tile(n, 1024)
    return pl.pallas_call(
        _ada_kernel,
        grid=(n // tn,),
        in_specs=[pl.BlockSpec((m, d), lambda j: (0, 0)),
                  pl.BlockSpec((d, tn), lambda j: (0, j)),
                  pl.BlockSpec((1, tn), lambda j: (0, j))],
        out_specs=pl.BlockSpec((m, tn), lambda j: (0, j)),
        out_shape=jax.ShapeDtypeStruct((m, n), F32),
        compiler_params=_cparams("arbitrary"),
        name="ada",
    )(c_all, w_ada, b_ada.reshape(1, n))


def _rms_mod(x, g, sc, sh):
    y = x * lax.rsqrt(jnp.mean(x * x, axis=-1, keepdims=True) + EPS)
    return (y * g) * (1.0 + sc) + sh


def _prenorm_kernel(x_ref, g_ref, sc_ref, sh_ref, o_ref):
    o_ref[...] = _rms_mod(x_ref[...], g_ref[...], sc_ref[...], sh_ref[...]).astype(o_ref.dtype)


def _mod_spec(mod, tm, rows_per_group):
    r, d = mod.shape[1], mod.shape[2]
    return pl.BlockSpec((None, r, d), lambda i: ((i * tm) // rows_per_group, 0, 0))


def _prenorm(x, g, sc, sh, tm, rows_per_group):
    n, d = x.shape
    return pl.pallas_call(
        _prenorm_kernel,
        grid=(n // tm,),
        in_specs=[pl.BlockSpec((tm, d), lambda i: (i, 0)),
                  pl.BlockSpec((1, d), lambda i: (0, 0)),
                  _mod_spec(sc, tm, rows_per_group),
                  _mod_spec(sh, tm, rows_per_group)],
        out_specs=pl.BlockSpec((tm, d), lambda i: (i, 0)),
        out_shape=jax.ShapeDtypeStruct((n, d), BF16),
        compiler_params=_cparams("arbitrary"),
        name="prenorm",
    )(x, g.reshape(1, d), sc, sh)


def _rope_tables(pos, head_dim):
    rd = head_dim // ROPE_FRAC
    half = rd // 2
    inv = ROPE_THETA ** (-jnp.arange(half, dtype=F32) / half)
    ang = pos.astype(F32)[:, None] * inv[None, :]
    cos, sin = jnp.cos(ang), jnp.sin(ang)
    t = pos.shape[0]
    rest = head_dim - rd
    c = jnp.concatenate([cos, cos, jnp.ones((t, rest), F32)], axis=-1)
    sa = jnp.concatenate([-sin, jnp.zeros((t, half + rest), F32)], axis=-1)
    sb = jnp.concatenate([jnp.zeros((t, half), F32), sin, jnp.zeros((t, rest), F32)], axis=-1)
    reps = LANES // head_dim
    return tuple(jnp.tile(a, (1, reps)) for a in (c, sa, sb)), half


def _rope(y, c, sa, sb, half):
    return y * c + pltpu.roll(y, LANES - half, 1) * sa + pltpu.roll(y, half, 1) * sb


def _head_norm(chunk, gain):
    y = chunk * lax.rsqrt(jnp.mean(chunk * chunk, axis=-1, keepdims=True) + EPS)
    return y * gain


def _qkv_kernel(*refs, n_q, n_k, rope_half, with_bf16):
    h_ref, w_ref, gain_ref = refs[:3]
    pos = 3
    if rope_half:
        c_ref, sa_ref, sb_ref = refs[3:6]
        pos = 6
    q_ref, k_ref, v_ref = refs[pos:pos + 3]
    z = _dot(h_ref[...], w_ref[...])
    for ch in range(n_q + n_k):
        sl = slice(ch * LANES, (ch + 1) * LANES)
        y = _head_norm(z[:, sl], gain_ref[:, sl])
        if rope_half:
            y = _rope(y, c_ref[...], sa_ref[...], sb_ref[...], rope_half)
        if ch < n_q:
            q_ref[:, sl] = y.astype(q_ref.dtype)
        else:
            k_ref[:, (ch - n_q) * LANES:(ch - n_q + 1) * LANES] = y
    v = z[:, (n_q + n_k) * LANES:]
    v_ref[...] = v
    if with_bf16:
        kb_ref, vb_ref = refs[pos + 3:pos + 5]
        kb_ref[...] = k_ref[...].astype(BF16)
        vb_ref[...] = v.astype(BF16)


def _qkv_proj(h, w, gain, tables, rope_half, tm, n_q, n_k, with_bf16):
    n, d = h.shape
    wq, wk = n_q * LANES, n_k * LANES
    width = w.shape[1]
    row = lambda i: (i, 0)
    const = lambda i: (0, 0)
    in_specs = [pl.BlockSpec((tm, d), row), pl.BlockSpec((d, width), const),
                pl.BlockSpec((1, wq + wk), const)]
    args = [h, w, gain]
    if rope_half:
        nt = tables[0].shape[0] // tm
        for t in tables:
            in_specs.append(pl.BlockSpec((tm, LANES), lambda i: (i % nt, 0)))
            args.append(t)
    out_shape = [jax.ShapeDtypeStruct((n, wq), BF16), jax.ShapeDtypeStruct((n, wk), F32),
                 jax.ShapeDtypeStruct((n, wk), F32)]
    out_specs = [pl.BlockSpec((tm, wq), row), pl.BlockSpec((tm, wk), row), pl.BlockSpec((tm, wk), row)]
    if with_bf16:
        out_shape += [jax.ShapeDtypeStruct((n, wk), BF16)] * 2
        out_specs += [pl.BlockSpec((tm, wk), row)] * 2
    return pl.pallas_call(
        functools.partial(_qkv_kernel, n_q=n_q, n_k=n_k, rope_half=rope_half, with_bf16=with_bf16),
        grid=(n // tm,), in_specs=in_specs, out_specs=out_specs, out_shape=out_shape,
        compiler_params=_cparams("arbitrary"), name="qkv_proj",
    )(*args)


def _qidx_kernel(h_ref, w_ref, c_ref, sa_ref, sb_ref, o_ref, *, half):
    z = _dot(h_ref[...], w_ref[...])
    for ch in range(IDX_HEADS // 2):
        y = _rope(z[:, ch * LANES:(ch + 1) * LANES], c_ref[...], sa_ref[...], sb_ref[...], half)
        o_ref[2 * ch] = y[:, :IDX_DIM].astype(o_ref.dtype)
        o_ref[2 * ch + 1] = y[:, IDX_DIM:].astype(o_ref.dtype)


def _qidx_proj(h, w, tables, half, tm):
    n, d = h.shape
    nt = tables[0].shape[0] // tm
    tab = pl.BlockSpec((tm, LANES), lambda i: (i % nt, 0))
    return pl.pallas_call(
        functools.partial(_qidx_kernel, half=half),
        grid=(n // tm,),
        in_specs=[pl.BlockSpec((tm, d), lambda i: (i, 0)),
                  pl.BlockSpec((d, IDX_HEADS * IDX_DIM), lambda i: (0, 0)), tab, tab, tab],
        out_specs=pl.BlockSpec((IDX_HEADS, tm, IDX_DIM), lambda i: (0, i, 0)),
        out_shape=jax.ShapeDtypeStruct((IDX_HEADS, n, IDX_DIM), BF16),
        compiler_params=_cparams("arbitrary"), name="qidx_proj",
    )(h, w, *tables)


def _misc_kernel(h_ref, w_ref, bf_ref, c_ref, sa_ref, sb_ref, logf_ref, ki_ref, wi_ref, *, half):
    z = _dot(h_ref[...], w_ref[...])
    f = z[:, :A_HEADS] + bf_ref[...]
    logf_ref[...] = jnp.minimum(f, 0.0) - jnp.log1p(jnp.exp(-jnp.abs(f)))
    y = _rope(z[:, LANES:2 * LANES], c_ref[...], sa_ref[...], sb_ref[...], half)
    ki_ref[...] = y[:, :IDX_DIM]
    wi_ref[...] = z[:, 2 * LANES:2 * LANES + IDX_HEADS]


def _misc_proj(h, w, b_fgate, tables, half, tm):
    n, d = h.shape
    nt = tables[0].shape[0] // tm
    tab = pl.BlockSpec((tm, LANES), lambda i: (i % nt, 0))
    row = lambda i: (i, 0)
    return pl.pallas_call(
        functools.partial(_misc_kernel, half=half),
        grid=(n // tm,),
        in_specs=[pl.BlockSpec((tm, d), row), pl.BlockSpec((d, 3 * LANES), lambda i: (0, 0)),
                  pl.BlockSpec((1, A_HEADS), lambda i: (0, 0)), tab, tab, tab],
        out_specs=[pl.BlockSpec((tm, A_HEADS), row), pl.BlockSpec((tm, IDX_DIM), row),
                   pl.BlockSpec((tm, IDX_HEADS), row)],
        out_shape=[jax.ShapeDtypeStruct((n, A_HEADS), F32), jax.ShapeDtypeStruct((n, IDX_DIM), F32),
                   jax.ShapeDtypeStruct((n, IDX_HEADS), F32)],
        compiler_params=_cparams("arbitrary"), name="misc_proj",
    )(h, w, b_fgate.reshape(1, A_HEADS), *tables)


def _gate_kernel(h_ref, w_ref, o_ref):
    o_ref[...] = _sigmoid(_dot(h_ref[...], w_ref[...]))


def _gate_proj(h, w, tm):
    n, d = h.shape
    width = w.shape[1]
    return pl.pallas_call(
        _gate_kernel, grid=(n // tm,),
        in_specs=[pl.BlockSpec((tm, d), lambda i: (i, 0)), pl.BlockSpec((d, width), lambda i: (0, 0))],
        out_specs=pl.BlockSpec((tm, width), lambda i: (i, 0)),
        out_shape=jax.ShapeDtypeStruct((n, width), F32),
        compiler_params=_cparams("arbitrary"), name="gate_proj",
    )(h, w)


def _cumsum_kernel(x_ref, o_ref, *, tc):
    t = x_ref.shape[-1]
    r = lax.broadcasted_iota(I32, (tc, tc), 0)
    c = lax.broadcasted_iota(I32, (tc, tc), 1)
    tri = jnp.where(r <= c, 1.0, 0.0).astype(F32)
    carry = jnp.zeros((x_ref.shape[0], 1), F32)
    for blk in range(t // tc):
        sl = slice(blk * tc, (blk + 1) * tc)
        cs = jnp.dot(x_ref[:, sl], tri, precision=lax.Precision.HIGHEST, preferred_element_type=F32) + carry
        o_ref[:, sl] = cs
        carry = cs[:, tc - 1:tc]


def _cumsum_lanes(x):
    b, h, t = x.shape
    tc = _tile(t, 512)
    return pl.pallas_call(
        functools.partial(_cumsum_kernel, tc=tc), grid=(b,),
        in_specs=[pl.BlockSpec((None, h, t), lambda i: (i, 0, 0))],
        out_specs=pl.BlockSpec((None, h, t), lambda i: (i, 0, 0)),
        out_shape=jax.ShapeDtypeStruct((b, h, t), F32),
        compiler_params=_cparams("arbitrary"), name="cumsum",
    )(x)


def _online_update(s, v16, m_ref, l_ref, acc_ref):
    m_prev = m_ref[...]
    m_new = jnp.maximum(m_prev, jnp.max(s, axis=-1, keepdims=True))
    alpha = jnp.exp(m_prev - m_new)
    p = jnp.exp(s - m_new)
    l_ref[...] = alpha * l_ref[...] + jnp.sum(p, axis=-1, keepdims=True)
    acc_ref[...] = alpha * acc_ref[...] + _dot(p.astype(BF16), v16)
    m_ref[...] = m_new


def _fox_kernel(q_ref, k_ref, v_ref, fq_ref, fk_ref, o_ref, m_sc, l_sc, acc_sc, *, tq, tk):
    i = pl.program_id(2)
    kk = pl.program_id(3)
    g_heads = q_ref.shape[1] // LANES

    @pl.when(kk == 0)
    def _():
        m_sc[...] = jnp.full(m_sc.shape, NEG, F32)
        l_sc[...] = jnp.zeros(l_sc.shape, F32)
        acc_sc[...] = jnp.zeros(acc_sc.shape, F32)

    @pl.when(kk <= i)
    def _():
        k16 = k_ref[...].astype(BF16)
        v16 = v_ref[...].astype(BF16)
        row = i * tq + lax.broadcasted_iota(I32, (tq, tk), 0)
        col = kk * tk + lax.broadcasted_iota(I32, (tq, tk), 1)
        causal = col <= row
        for g in range(g_heads):
            s = _dot_nt(q_ref[:, g * LANES:(g + 1) * LANES], k16) * ATTN_SCALE
            s = s + fq_ref[:, g:g + 1] - fk_ref[g:g + 1, :]
            s = jnp.where(causal, s, NEG)
            _online_update(s, v16, m_sc.at[g], l_sc.at[g], acc_sc.at[g])

    @pl.when(kk == pl.num_programs(3) - 1)
    def _():
        for g in range(g_heads):
            o_ref[:, g * LANES:(g + 1) * LANES] = (acc_sc[g] / l_sc[g]).astype(o_ref.dtype)


def _fox_prompt(q, k, v, fq, fk, bsz, t):
    g = A_HEADS // A_KV_HEADS
    tq = tk = _tile(t, 512)
    nq = t // tq
    return pl.pallas_call(
        functools.partial(_fox_kernel, tq=tq, tk=tk),
        grid=(bsz, A_KV_HEADS, nq, nq),
        in_specs=[pl.BlockSpec((tq, g * LANES), lambda b, j, i, kk: (b * nq + i, j)),
                  pl.BlockSpec((tk, LANES), lambda b, j, i, kk: (b * nq + jnp.minimum(kk, i), j)),
                  pl.BlockSpec((tk, LANES), lambda b, j, i, kk: (b * nq + jnp.minimum(kk, i), j)),
                  pl.BlockSpec((None, None, tq, g), lambda b, j, i, kk: (b, j, i, 0)),
                  pl.BlockSpec((None, None, g, tk), lambda b, j, i, kk: (b, j, 0, jnp.minimum(kk, i)))],
        out_specs=pl.BlockSpec((tq, g * LANES), lambda b, j, i, kk: (b * nq + i, j)),
        out_shape=jax.ShapeDtypeStruct((bsz * t, A_HEADS * LANES), BF16),
        scratch_shapes=[pltpu.VMEM((g, tq, 1), F32), pltpu.VMEM((g, tq, 1), F32),
                        pltpu.VMEM((g, tq, LANES), F32)],
        compiler_params=_cparams("arbitrary", "arbitrary", "arbitrary", "arbitrary"),
        name="fox_prompt",
    )(q, k, v, fq, fk)


def _kth_key(count_ge, shape, k):
    def body(bi, v):
        cand = v + lax.shift_left(jnp.int32(1), 31 - bi)
        return jnp.where(count_ge(cand) >= k, cand, v)
    return lax.fori_loop(0, 32, body, jnp.full(shape, INT_MIN, I32))


def _dsa_kernel(qb_ref, qi_ref, wi_ref, ki_ref, kb_ref, vb_ref, o_ref, key_sc, m_sc, l_sc, acc_sc,
                *, tq, topk):
    i = pl.program_id(1)
    nch = i + 1
    row = i * tq + lax.broadcasted_iota(I32, (tq, 1), 0)

    def scores(c, carry):
        kc = ki_ref[pl.ds(pl.multiple_of(c * tq, tq), tq), :].astype(BF16)
        acc = jnp.zeros((tq, tq), F32)
        for h in range(IDX_HEADS):
            acc = acc + wi_ref[:, h:h + 1] * jnp.maximum(_dot_nt(qi_ref[h], kc), 0.0)
        col = c * tq + lax.broadcasted_iota(I32, (1, tq), 1)
        key_sc[c] = jnp.where(col <= row, _float_key(acc), MASKED_KEY)
        return carry

    lax.fori_loop(0, nch, scores, 0)

    def count_ge(cand):
        def cbody(c, cnt):
            ge = jnp.where(key_sc[c] >= cand, 1.0, 0.0)
            for s in range(tq // LANES):
                cnt = cnt + ge[:, s * LANES:(s + 1) * LANES]
            return cnt
        cnt = lax.fori_loop(0, nch, cbody, jnp.zeros((tq, LANES), F32))
        return jnp.sum(cnt, axis=-1, keepdims=True)

    thr = lax.cond((i + 1) * tq > topk,
                   lambda: _kth_key(count_ge, (tq, 1), float(topk)),
                   lambda: jnp.full((tq, 1), INT_MIN, I32))
    thr = jnp.maximum(thr, MASKED_KEY + 1)

    m_sc[...] = jnp.full(m_sc.shape, NEG, F32)
    l_sc[...] = jnp.zeros(l_sc.shape, F32)
    acc_sc[...] = jnp.zeros(acc_sc.shape, F32)

    def attend(c, carry):
        bias = jnp.where(key_sc[c] >= thr, 0.0, NEG)
        rows = pl.ds(pl.multiple_of(c * tq, tq), tq)
        for h in range(B_HEADS):
            j = h // (B_HEADS // B_KV_HEADS)
            k16 = kb_ref[rows, j * LANES:(j + 1) * LANES]
            v16 = vb_ref[rows, j * LANES:(j + 1) * LANES]
            s = _dot_nt(qb_ref[:, h * LANES:(h + 1) * LANES], k16) * ATTN_SCALE + bias
            _online_update(s, v16, m_sc.at[h], l_sc.at[h], acc_sc.at[h])
        return carry

    lax.fori_loop(0, nch, attend, 0)
    for h in range(B_HEADS):
        o_ref[:, h * LANES:(h + 1) * LANES] = (acc_sc[h] / l_sc[h]).astype(o_ref.dtype)


def _dsa_prompt(qb, qi, wi, ki, kb16, vb16, bsz, t):
    tq = _tile(t, 256)
    nq = t // tq
    topk = min(TOPK_MAX, t // 4)
    kvw = B_KV_HEADS * LANES
    return pl.pallas_call(
        functools.partial(_dsa_kernel, tq=tq, topk=topk),
        grid=(bsz, nq),
        in_specs=[pl.BlockSpec((tq, B_HEADS * LANES), lambda b, i: (b * nq + i, 0)),
                  pl.BlockSpec((IDX_HEADS, tq, IDX_DIM), lambda b, i: (0, b * nq + i, 0)),
                  pl.BlockSpec((tq, IDX_HEADS), lambda b, i: (b * nq + i, 0)),
                  pl.BlockSpec((t, IDX_DIM), lambda b, i: (b, 0)),
                  pl.BlockSpec((t, kvw), lambda b, i: (b, 0)),
                  pl.BlockSpec((t, kvw), lambda b, i: (b, 0))],
        out_specs=pl.BlockSpec((tq, B_HEADS * LANES), lambda b, i: (b * nq + i, 0)),
        out_shape=jax.ShapeDtypeStruct((bsz * t, B_HEADS * LANES), BF16),
        scratch_shapes=[pltpu.VMEM((nq, tq, tq), I32),
                        pltpu.VMEM((B_HEADS, tq, 1), F32), pltpu.VMEM((B_HEADS, tq, 1), F32),
                        pltpu.VMEM((B_HEADS, tq, LANES), F32)],
        compiler_params=_cparams("arbitrary", "arbitrary"),
        name="dsa_prompt",
    )(qb, qi, wi, ki, kb16, vb16)


def _page_specs(block, n_rep):
    def spec(r):
        return pl.BlockSpec(block, lambda b, pg, pt: (pt[b, pg * n_rep + r],) + (0,) * (len(block) - 1))
    return [spec(r) for r in range(n_rep)]


def _fscan_kernel(pt_ref, *refs, n_rep):
    pages = refs[:n_rep]
    new_ref, fp_ref, fn_ref, carry_sc = refs[n_rep:]
    pg = pl.program_id(1)
    r = lax.broadcasted_iota(I32, (PAGE_SIZE, PAGE_SIZE), 0)
    c = lax.broadcasted_iota(I32, (PAGE_SIZE, PAGE_SIZE), 1)
    tri = jnp.where(r <= c, 1.0, 0.0).astype(F32)

    def cs_of(x, carry):
        return jnp.dot(x, tri, precision=lax.Precision.HIGHEST, preferred_element_type=F32) + carry

    @pl.when(pg == 0)
    def _():
        carry_sc[...] = jnp.zeros(carry_sc.shape, F32)

    carry = carry_sc[...]
    for rr in range(n_rep):
        cs = cs_of(pages[rr][...], carry)
        fp_ref[:, rr * PAGE_SIZE:(rr + 1) * PAGE_SIZE] = cs
        carry = cs[:, PAGE_SIZE - 1:PAGE_SIZE]
    carry_sc[...] = carry

    @pl.when(pg == pl.num_programs(1) - 1)
    def _():
        fn_ref[...] = cs_of(new_ref[...], carry)


def _fscan(page_table, logf_pool_t, logf_new_t, n_rep):
    bsz, n_pages = page_table.shape
    h = logf_pool_t.shape[1]
    grid_spec = pltpu.PrefetchScalarGridSpec(
        num_scalar_prefetch=1, grid=(bsz, n_pages // n_rep),
        in_specs=_page_specs((None, h, PAGE_SIZE), n_rep)
        + [pl.BlockSpec((None, h, PAGE_SIZE), lambda b, pg, pt: (b, 0, 0))],
        out_specs=[pl.BlockSpec((None, h, n_rep * PAGE_SIZE), lambda b, pg, pt: (b, 0, pg)),
                   pl.BlockSpec((None, h, PAGE_SIZE), lambda b, pg, pt: (b, 0, 0))],
        scratch_shapes=[pltpu.VMEM((h, 1), F32)])
    return pl.pallas_call(
        functools.partial(_fscan_kernel, n_rep=n_rep), grid_spec=grid_spec,
        out_shape=[jax.ShapeDtypeStruct((bsz, h, n_pages * PAGE_SIZE), F32),
                   jax.ShapeDtypeStruct((bsz, h, PAGE_SIZE), F32)],
        compiler_params=_cparams("arbitrary", "arbitrary"), name="fscan",
    )(page_table, *([logf_pool_t] * n_rep), logf_new_t)


def _sidx_kernel(pt_ref, *refs, n_rep, n_tok, topk):
    pages = refs[:n_rep]
    qi_ref, wi_ref, kn_ref, kb_ref, nb_ref, key_sc = refs[n_rep:]
    pg = pl.program_id(1)
    npg = pl.num_programs(1)
    rows = 8
    pad = jnp.full((rows - n_tok, PAGE_SIZE), MASKED_KEY, I32)

    def score(keys_f32):
        d = jnp.maximum(_dot_nt(qi_ref[...], keys_f32.astype(BF16)), 0.0) * wi_ref[...]
        return jnp.sum(d.reshape(n_tok, IDX_HEADS, PAGE_SIZE), axis=1)

    for rr in range(n_rep):
        key_sc[pg * n_rep + rr] = jnp.concatenate([_float_key(score(pages[rr][...])), pad], axis=0)

    @pl.when(pg == npg - 1)
    def _():
        n_chunks = npg * n_rep
        t_idx = lax.broadcasted_iota(I32, (n_tok, PAGE_SIZE), 0)
        s_idx = lax.broadcasted_iota(I32, (n_tok, PAGE_SIZE), 1)
        new = jnp.where(s_idx <= t_idx, _float_key(score(kn_ref[...])), MASKED_KEY)
        key_sc[n_chunks] = jnp.concatenate([new, pad], axis=0)

        def count_ge(cand):
            def cbody(c, cnt):
                return cnt + jnp.where(key_sc[c] >= cand, 1.0, 0.0)
            cnt = lax.fori_loop(0, n_chunks + 1, cbody, jnp.zeros((rows, PAGE_SIZE), F32))
            return jnp.sum(cnt, axis=-1, keepdims=True)

        thr = jnp.maximum(_kth_key(count_ge, (rows, 1), float(topk)), MASKED_KEY + 1)

        def wbody(c, carry):
            kb_ref[c] = jnp.where(key_sc[c] >= thr, 0.0, NEG)
            return carry
        lax.fori_loop(0, n_chunks, wbody, 0)
        nb_ref[...] = jnp.where(key_sc[n_chunks] >= thr, 0.0, NEG)


def _sample_index(page_table, idx_pool, qi, wi, ki_new, n_rep, n_tok):
    bsz, n_pages = page_table.shape
    p = n_pages * PAGE_SIZE
    topk = min(TOPK_MAX, (p + n_tok) // 4)
    nr = n_tok * IDX_HEADS
    grid_spec = pltpu.PrefetchScalarGridSpec(
        num_scalar_prefetch=1, grid=(bsz, n_pages // n_rep),
        in_specs=_page_specs((None, PAGE_SIZE, IDX_DIM), n_rep)
        + [pl.BlockSpec((None, nr, IDX_DIM), lambda b, pg, pt: (b, 0, 0)),
           pl.BlockSpec((None, nr, 1), lambda b, pg, pt: (b, 0, 0)),
           pl.BlockSpec((None, PAGE_SIZE, IDX_DIM), lambda b, pg, pt: (b, 0, 0))],
        out_specs=[pl.BlockSpec((None, n_pages, 8, PAGE_SIZE), lambda b, pg, pt: (b, 0, 0, 0)),
                   pl.BlockSpec((None, 8, PAGE_SIZE), lambda b, pg, pt: (b, 0, 0))],
        scratch_shapes=[pltpu.VMEM((n_pages + 1, 8, PAGE_SIZE), I32)])
    return pl.pallas_call(
        functools.partial(_sidx_kernel, n_rep=n_rep, n_tok=n_tok, topk=topk), grid_spec=grid_spec,
        out_shape=[jax.ShapeDtypeStruct((bsz, n_pages, 8, PAGE_SIZE), F32),
                   jax.ShapeDtypeStruct((bsz, 8, PAGE_SIZE), F32)],
        compiler_params=_cparams("arbitrary", "arbitrary"), name="sample_index",
    )(page_table, *([idx_pool] * n_rep), qi, wi, ki_new)


def _paged_attn_kernel(pt_ref, *refs, n_rep, per_head):
    kp = refs[:n_rep]
    vp = refs[n_rep:2 * n_rep]
    q_ref, kb_ref, rb_ref, kn_ref, vn_ref, nb_ref, o_ref, m_sc, l_sc, acc_sc = refs[2 * n_rep:]
    pg = pl.program_id(1)
    nrow = q_ref.shape[0]

    @pl.when(pg == 0)
    def _():
        m_sc[...] = jnp.full(m_sc.shape, NEG, F32)
        l_sc[...] = jnp.zeros(l_sc.shape, F32)
        acc_sc[...] = jnp.zeros(acc_sc.shape, F32)

    def expand(kb):
        if per_head:
            return jnp.concatenate([kb] * (nrow // kb.shape[0]), axis=0)
        return jnp.concatenate([jnp.broadcast_to(kb[t:t + 1], (8, kb.shape[1])) for t in range(nrow // 8)], axis=0)

    q = q_ref[...]
    rb = rb_ref[...]

    def update(s, vs):
        m_prev = m_sc[...]
        m_new = jnp.maximum(m_prev, jnp.max(s, axis=-1, keepdims=True))
        alpha = jnp.exp(m_prev - m_new)
        p = jnp.exp(s - m_new)
        l_sc[...] = alpha * l_sc[...] + jnp.sum(p, axis=-1, keepdims=True)
        acc = alpha * acc_sc[...]
        for r, v in enumerate(vs):
            acc = acc + _dot(p[:, r * PAGE_SIZE:(r + 1) * PAGE_SIZE].astype(BF16), v.astype(BF16))
        acc_sc[...] = acc
        m_sc[...] = m_new

    s = jnp.concatenate([_dot_nt(q, kp[r][...].astype(BF16)) for r in range(n_rep)], axis=1)
    update(s * ATTN_SCALE + rb + expand(kb_ref[...]), [vp[r][...] for r in range(n_rep)])

    @pl.when(pg == pl.num_programs(1) - 1)
    def _():
        sn = _dot_nt(q, kn_ref[...].astype(BF16)) * ATTN_SCALE + rb + nb_ref[...]
        update(sn, [vn_ref[...]])
        o_ref[...] = acc_sc[...] / l_sc[...]


def _paged_attn(page_table, k_pool, v_pool, q_bd, key_bias, row_bias, k_new, v_new, new_bias, n_rep, per_head):
    bsz, n_pages = page_table.shape
    nrow, w = q_bd.shape[1], q_bd.shape[2]
    kbh = key_bias.shape[1]
    const = lambda b, pg, pt: (b, 0, 0)
    grid_spec = pltpu.PrefetchScalarGridSpec(
        num_scalar_prefetch=1, grid=(bsz, n_pages // n_rep),
        in_specs=_page_specs((None, PAGE_SIZE, w), n_rep) + _page_specs((None, PAGE_SIZE, w), n_rep)
        + [pl.BlockSpec((None, nrow, w), const),
           pl.BlockSpec((None, kbh, n_rep * PAGE_SIZE), lambda b, pg, pt: (b, 0, pg)),
           pl.BlockSpec((None, nrow, 1), const),
           pl.BlockSpec((None, PAGE_SIZE, w), const),
           pl.BlockSpec((None, PAGE_SIZE, w), const),
           pl.BlockSpec((None, nrow, PAGE_SIZE), const)],
        out_specs=pl.BlockSpec((None, nrow, w), const),
        scratch_shapes=[pltpu.VMEM((nrow, 1), F32), pltpu.VMEM((nrow, 1), F32), pltpu.VMEM((nrow, w), F32)])
    return pl.pallas_call(
        functools.partial(_paged_attn_kernel, n_rep=n_rep, per_head=per_head), grid_spec=grid_spec,
        out_shape=jax.ShapeDtypeStruct((bsz, nrow, w), F32),
        compiler_params=_cparams("arbitrary", "arbitrary"), name="paged_attn",
    )(page_table, *([k_pool] * n_rep), *([v_pool] * n_rep), q_bd, key_bias, row_bias, k_new, v_new, new_bias)


def _merge_kernel(oa_ref, ob_ref, ga_ref, gb_ref, wa_ref, wb_ref, o_ref):
    m = ga_ref[...] * _dot(oa_ref[...], wa_ref[...]) + gb_ref[...] * _dot(ob_ref[...], wb_ref[...])
    o_ref[...] = m.astype(o_ref.dtype)


def _merge(oa, ob, ga, gb, wa, wb, tm):
    n, e = oa.shape
    d = wa.shape[1]
    row = lambda i: (i, 0)
    const = lambda i: (0, 0)
    return pl.pallas_call(
        _merge_kernel, grid=(n // tm,),
        in_specs=[pl.BlockSpec((tm, e), row), pl.BlockSpec((tm, e), row),
                  pl.BlockSpec((tm, d), row), pl.BlockSpec((tm, d), row),
                  pl.BlockSpec((e, d), const), pl.BlockSpec((e, d), const)],
        out_specs=pl.BlockSpec((tm, d), row),
        out_shape=jax.ShapeDtypeStruct((n, d), BF16),
        compiler_params=_cparams("arbitrary"), name="merge",
    )(oa, ob, ga, gb, wa, wb)


def _outproj_kernel(x_ref, m_ref, w_ref, g1_ref, gn_ref, sc_ref, sh_ref, x1_ref, h2_ref):
    x1 = x_ref[...] + g1_ref[...] * _dot(m_ref[...], w_ref[...])
    x1_ref[...] = x1
    h2_ref[...] = _rms_mod(x1, gn_ref[...], sc_ref[...], sh_ref[...]).astype(h2_ref.dtype)


def _outproj(x, merged, w_out, ga1, g_norm2, sc2, sh2, tm, rows_per_group):
    n, d = x.shape
    row = lambda i: (i, 0)
    return pl.pallas_call(
        _outproj_kernel, grid=(n // tm,),
        in_specs=[pl.BlockSpec((tm, d), row), pl.BlockSpec((tm, d), row),
                  pl.BlockSpec((d, d), lambda i: (0, 0)),
                  _mod_spec(ga1, tm, rows_per_group),
                  pl.BlockSpec((1, d), lambda i: (0, 0)),
                  _mod_spec(sc2, tm, rows_per_group), _mod_spec(sh2, tm, rows_per_group)],
        out_specs=[pl.BlockSpec((tm, d), row), pl.BlockSpec((tm, d), row)],
        out_shape=[jax.ShapeDtypeStruct((n, d), F32), jax.ShapeDtypeStruct((n, d), BF16)],
        compiler_params=_cparams("arbitrary"), name="outproj",
    )(x, merged, w_out, ga1, g_norm2.reshape(1, d), sc2, sh2)


def _top_values(x, n):
    out = []
    for _ in range(n):
        m = jnp.max(x, axis=0, keepdims=True)
        out.append(m)
        x = jnp.where(x == m, -jnp.inf, x)
    return out


def _peer_query_kernel(ht_ref, wq_ref, sk_ref, s_ref, thr_ref, m1_ref, m2_ref, rz_ref, sv_sc):
    qt = _dot(wq_ref[...], ht_ref[...])
    n_hc = sk_ref.shape[0]
    for hc in range(n_hc):
        s = _dot(sk_ref[hc], qt[hc * PEER_KEYS:(hc + 1) * PEER_KEYS, :].astype(BF16))
        s_ref[hc] = s
        for r, m in enumerate(_top_values(s, PEER_TOPK)):
            sv_sc[hc, r:r + 1, :] = m
    for h in range(n_hc // 2):
        sv1 = sv_sc[2 * h]
        sv2 = sv_sc[2 * h + 1]
        cands = [sv1[0:1] + sv2]
        cands += [sv1[a:a + 1] + sv2[0:8] for a in range(1, 8)]
        cands += [sv1[8:16] + sv2[0:1]]
        tops = _top_values(jnp.concatenate(cands, axis=0), PEER_TOPK)
        z = jnp.zeros_like(tops[0])
        for m in tops:
            z = z + jnp.exp(m - tops[0])
        thr_ref[h:h + 1, :] = tops[-1]
        m1_ref[h:h + 1, :] = sv1[0:1]
        m2_ref[h:h + 1, :] = sv2[0:1]
        rz_ref[h:h + 1, :] = 1.0 / z


def _peer_query(h2t, wqt, subkeys, tm):
    d, n = h2t.shape
    n_hc = subkeys.shape[0]
    col = lambda i: (0, i)
    small = jax.ShapeDtypeStruct((PEER_HEADS, n), F32)
    small_spec = pl.BlockSpec((PEER_HEADS, tm), col)
    return pl.pallas_call(
        _peer_query_kernel, grid=(n // tm,),
        in_specs=[pl.BlockSpec((d, tm), col), pl.BlockSpec(wqt.shape, lambda i: (0, 0)),
                  pl.BlockSpec(subkeys.shape, lambda i: (0, 0, 0))],
        out_specs=[pl.BlockSpec((n_hc, PEER_KEYS, tm), lambda i: (0, 0, i))] + [small_spec] * 4,
        out_shape=[jax.ShapeDtypeStruct((n_hc, PEER_KEYS, n), F32)] + [small] * 4,
        scratch_shapes=[pltpu.VMEM((n_hc, PEER_TOPK, tm), F32)],
        compiler_params=_cparams("arbitrary"), name="peer_query",
    )(h2t, wqt, subkeys)


def _peer_dense_kernel(ht_ref, u_ref, vt_ref, s_ref, thr_ref, m1_ref, m2_ref, rz_ref, o_ref, e2_sc, *, ci1):
    e = pl.program_id(1)

    @pl.when(e == 0)
    def _():
        o_ref[...] = jnp.zeros(o_ref.shape, F32)
        for h in range(PEER_HEADS):
            e2_sc[h] = jnp.exp(s_ref[2 * h + 1] - m2_ref[h:h + 1, :]) * rz_ref[h:h + 1, :]

    a = _dot(u_ref[...], ht_ref[...])
    a = 0.5 * a * (1.0 + lax.erf(a * (2.0 ** -0.5)))
    blocks = []
    for q in range(ci1):
        i1 = e * ci1 + q
        w = jnp.zeros((PEER_KEYS, a.shape[1]), F32)
        for h in range(PEER_HEADS):
            s1 = s_ref[2 * h, pl.ds(i1, 1), :]
            sel = (s_ref[2 * h + 1] + s1) >= thr_ref[h:h + 1, :]
            w = w + jnp.where(sel, e2_sc[h] * jnp.exp(s1 - m1_ref[h:h + 1, :]), 0.0)
        blocks.append((w * a[q * PEER_KEYS:(q + 1) * PEER_KEYS, :]).astype(BF16))
    o_ref[...] += _dot(vt_ref[...], jnp.concatenate(blocks, axis=0))


def _peer_dense(h2t, u16, vt16, s_t, thr, m1, m2, rz, tm, te):
    d, n = h2t.shape
    n_exp = u16.shape[0]
    n_hc = s_t.shape[0]
    tok = lambda i, e: (0, i)
    small_spec = pl.BlockSpec((PEER_HEADS, tm), tok)
    return pl.pallas_call(
        functools.partial(_peer_dense_kernel, ci1=te // PEER_KEYS),
        grid=(n // tm, n_exp // te),
        in_specs=[pl.BlockSpec((d, tm), tok),
                  pl.BlockSpec((te, d), lambda i, e: (e, 0)),
                  pl.BlockSpec((d, te), lambda i, e: (0, e)),
                  pl.BlockSpec((n_hc, PEER_KEYS, tm), lambda i, e: (0, 0, i))] + [small_spec] * 4,
        out_specs=pl.BlockSpec((d, tm), tok),
        out_shape=jax.ShapeDtypeStruct((d, n), F32),
        scratch_shapes=[pltpu.VMEM((PEER_HEADS, PEER_KEYS, tm), F32)],
        compiler_params=_cparams("arbitrary", "arbitrary"), name="peer_dense",
    )(h2t, u16, vt16, s_t, thr, m1, m2, rz)


def _final_kernel(x_ref, pt_ref, g2_ref, o_ref):
    o_ref[...] = x_ref[...] + g2_ref[...] * pt_ref[...].T


def _final(x1, peer_t, ga2, tm, rows_per_group):
    n, d = x1.shape
    return pl.pallas_call(
        _final_kernel, grid=(n // tm,),
        in_specs=[pl.BlockSpec((tm, d), lambda i: (i, 0)), pl.BlockSpec((d, tm), lambda i: (0, i)),
                  _mod_spec(ga2, tm, rows_per_group)],
        out_specs=pl.BlockSpec((tm, d), lambda i: (i, 0)),
        out_shape=jax.ShapeDtypeStruct((n, d), F32),
        compiler_params=_cparams("arbitrary"), name="final",
    )(x1, peer_t, ga2)


def _split_w_in(w_in, d_model):
    sizes = (A_HEADS * HEAD_DIM, A_KV_HEADS * HEAD_DIM, A_KV_HEADS * HEAD_DIM, A_HEADS,
             B_HEADS * HEAD_DIM, B_KV_HEADS * HEAD_DIM, B_KV_HEADS * HEAD_DIM,
             IDX_HEADS * IDX_DIM, IDX_DIM, IDX_HEADS, d_model, d_model)
    offs = [0]
    for s in sizes:
        offs.append(offs[-1] + s)
    cols = lambda a, b: w_in[:, offs[a]:offs[b]].astype(BF16)
    padw = lambda a: jnp.pad(w_in[:, offs[a]:offs[a + 1]], ((0, 0), (0, LANES - sizes[a]))).astype(BF16)
    return dict(a=cols(0, 3), b=cols(4, 7), qi=cols(7, 8),
                misc=jnp.concatenate([padw(3), padw(8), padw(9)], axis=1),
                gate_a=cols(10, 11), gate_b=cols(11, 12))


def _front(x2, mods, pos, w, p, tm, rows_per_group):
    sh1, sc1 = mods[0], mods[1]
    h = _prenorm(x2, p["g_norm1"], sc1, sh1, tm, rows_per_group)
    tab128, half128 = _rope_tables(pos, HEAD_DIM)
    tab64, half64 = _rope_tables(pos, IDX_DIM)
    gain_a = jnp.concatenate([jnp.tile(p["g_qn_a"], A_HEADS), jnp.tile(p["g_kn_a"], A_KV_HEADS)]).reshape(1, -1)
    gain_b = jnp.concatenate([jnp.tile(p["g_qn_b"], B_HEADS), jnp.tile(p["g_kn_b"], B_KV_HEADS)]).reshape(1, -1)
    q_a, k_a, v_a = _qkv_proj(h, w["a"], gain_a, None, 0, tm, A_HEADS, A_KV_HEADS, False)
    q_b, k_b, v_b, kb16, vb16 = _qkv_proj(h, w["b"], gain_b, tab128, half128, tm, B_HEADS, B_KV_HEADS, True)
    q_i = _qidx_proj(h, w["qi"], tab64, half64, tm)
    logf, k_i, w_i = _misc_proj(h, w["misc"], p["b_fgate"], tab64, half64, tm)
    gate_a = _gate_proj(h, w["gate_a"], tm)
    gate_b = _gate_proj(h, w["gate_b"], tm)
    return dict(q_a=q_a, k_a=k_a, v_a=v_a, logf=logf, q_b=q_b, k_b=k_b, v_b=v_b, kb16=kb16, vb16=vb16,
                q_i=q_i, k_i=k_i, w_i=w_i, gate_a=gate_a, gate_b=gate_b)


def _back(x2, f, o_a, o_b, mods, p, tm, rows_per_group, tm_peer, te):
    ga1, sh2, sc2, ga2 = mods[2], mods[3], mods[4], mods[5]
    merged = _merge(o_a, o_b, f["gate_a"], f["gate_b"], p["w_branch_a"], p["w_branch_b"], tm)
    x1, h2 = _outproj(x2, merged, p["w_out"], ga1, p["g_norm2"], sc2, sh2, tm, rows_per_group)
    h2t = h2.T
    s_t, thr, m1, m2, rz = _peer_query(h2t, p["wq_t"], p["subkeys"], tm_peer)
    peer_t = _peer_dense(h2t, p["u16"], p["vt16"], s_t, thr, m1, m2, rz, tm_peer, te)
    return _final(x1, peer_t, ga2, tm, rows_per_group)


def _prompt_group(x, ada, w, p):
    bsz, t, d = x.shape
    tm = _tile(t, 512)
    mods = [ada[:, i].reshape(bsz, 1, d) for i in range(N_ADA)]
    x2 = x.reshape(bsz * t, d)
    f = _front(x2, mods, jnp.arange(t), w, p, tm, t)
    g = A_HEADS // A_KV_HEADS
    f_t = _cumsum_lanes(f["logf"].reshape(bsz, t, A_HEADS).transpose(0, 2, 1))
    fk = f_t.reshape(bsz, A_KV_HEADS, g, t)
    fq = fk.transpose(0, 1, 3, 2)
    o_a = _fox_prompt(f["q_a"], f["k_a"], f["v_a"], fq, fk, bsz, t)
    o_b = _dsa_prompt(f["q_b"], f["q_i"], f["w_i"], f["k_i"], f["kb16"], f["vb16"], bsz, t)
    y = _back(x2, f, o_a, o_b, mods, p, tm, t, _tile(bsz * t, 512), 512)
    return y.reshape(bsz, t, d), f


def _block_diag_q(q, bsz, n_tok, n_heads, n_kv):
    q4 = q.reshape(bsz, n_tok, n_heads, 1, HEAD_DIM)
    sel = (jnp.arange(n_heads)[:, None] // (n_heads // n_kv) == jnp.arange(n_kv)[None, :])
    out = jnp.where(sel[None, None, :, :, None], q4, jnp.zeros((), q.dtype))
    return out.reshape(bsz, n_tok * n_heads, n_kv * HEAD_DIM)


def _diag_out(o, bsz, n_tok, n_heads, n_kv):
    o5 = o.reshape(bsz, n_tok, n_kv, n_heads // n_kv, n_kv, HEAD_DIM)
    picked = jnp.stack([o5[:, :, j, :, j, :] for j in range(n_kv)], axis=2)
    return picked.reshape(bsz * n_tok, n_heads * HEAD_DIM)


def _pad_rows(a, rows):
    return jnp.pad(a, ((0, 0), (0, rows - a.shape[1]), (0, 0)))


def _sample_group(x, ada, w, p, caches, page_table):
    bsz, n_tok, d = x.shape
    n = bsz * n_tok
    ck_a, cv_a, clf, ck_b, cv_b, cki = caches
    n_pages = page_table.shape[1]
    past = n_pages * PAGE_SIZE
    tm = _tile(n, 512)
    mods = [jnp.repeat(ada[:, i], n_tok, axis=0).reshape(n // tm, tm, d) for i in range(N_ADA)]
    x2 = x.reshape(n, d)
    pos = jnp.tile(past + jnp.arange(n_tok), bsz)
    f = _front(x2, mods, pos, w, p, tm, tm)
    n_rep = 8 if n_pages % 8 == 0 else 1
    n_pool = ck_a.shape[0]

    lf_new = jnp.pad(f["logf"].reshape(bsz, n_tok, A_HEADS).transpose(0, 2, 1),
                     ((0, 0), (0, 0), (0, PAGE_SIZE - n_tok)))
    fp_t, fn_t = _fscan(page_table, clf.transpose(0, 2, 1), lf_new, n_rep)
    fn = fn_t[:, :, :n_tok].transpose(0, 2, 1)
    row_bias = fn.reshape(bsz, n_tok * A_HEADS, 1)
    t_idx = jnp.arange(n_tok)
    causal = t_idx[None, :] <= t_idx[:, None]
    nb = jnp.where(causal[None, :, None, :], -fn.transpose(0, 2, 1)[:, None, :, :], NEG)
    new_bias = jnp.pad(nb.reshape(bsz, n_tok * A_HEADS, n_tok), ((0, 0), (0, 0), (0, PAGE_SIZE - n_tok)),
                       constant_values=NEG)
    kvw = A_KV_HEADS * HEAD_DIM
    o_full = _paged_attn(page_table, ck_a.reshape(n_pool, PAGE_SIZE, kvw), cv_a.reshape(n_pool, PAGE_SIZE, kvw),
                         _block_diag_q(f["q_a"], bsz, n_tok, A_HEADS, A_KV_HEADS), -fp_t, row_bias,
                         _pad_rows(f["k_a"].reshape(bsz, n_tok, kvw), PAGE_SIZE),
                         _pad_rows(f["v_a"].reshape(bsz, n_tok, kvw), PAGE_SIZE), new_bias, n_rep, True)
    o_a = _diag_out(o_full, bsz, n_tok, A_HEADS, A_KV_HEADS).astype(BF16)

    qi = f["q_i"].reshape(IDX_HEADS, bsz, n_tok, IDX_DIM).transpose(1, 2, 0, 3).reshape(bsz, n_tok * IDX_HEADS, IDX_DIM)
    wi = f["w_i"].reshape(bsz, n_tok * IDX_HEADS, 1)
    kb, nbm = _sample_index(page_table, cki, qi, wi, _pad_rows(f["k_i"].reshape(bsz, n_tok, IDX_DIM), PAGE_SIZE),
                            n_rep, n_tok)
    key_bias = kb.transpose(0, 2, 1, 3).reshape(bsz, 8, past)
    new_bias_b = jnp.repeat(nbm[:, :n_tok], B_HEADS, axis=1)
    kvw = B_KV_HEADS * HEAD_DIM
    o_full = _paged_attn(page_table, ck_b.reshape(n_pool, PAGE_SIZE, kvw), cv_b.reshape(n_pool, PAGE_SIZE, kvw),
                         _block_diag_q(f["q_b"], bsz, n_tok, B_HEADS, B_KV_HEADS), key_bias,
                         jnp.zeros((bsz, n_tok * B_HEADS, 1), F32),
                         _pad_rows(f["k_b"].reshape(bsz, n_tok, kvw), PAGE_SIZE),
                         _pad_rows(f["v_b"].reshape(bsz, n_tok, kvw), PAGE_SIZE), new_bias_b, n_rep, False)
    o_b = _diag_out(o_full, bsz, n_tok, B_HEADS, B_KV_HEADS).astype(BF16)

    y = _back(x2, f, o_a, o_b, mods, p, tm, tm, _tile(n, 512), 512)
    return y.reshape(bsz, n_tok, d), f


def _layer_outputs(f, bsz, t):
    return (f["k_a"].reshape(bsz, t, A_KV_HEADS, HEAD_DIM), f["v_a"].reshape(bsz, t, A_KV_HEADS, HEAD_DIM),
            f["logf"].reshape(bsz, t, A_HEADS),
            f["k_b"].reshape(bsz, t, B_KV_HEADS, HEAD_DIM), f["v_b"].reshape(bsz, t, B_KV_HEADS, HEAD_DIM),
            f["k_i"].reshape(bsz, t, IDX_DIM))


def kernel(x_prompt, x_sample, cache_fox_k, cache_fox_v, cache_fox_logf, cache_dsa_k, cache_dsa_v, cache_idx_k, page_table, c_prompt, c_sample, w_ada, b_ada, g_norm1, g_norm2, w_in, b_fgate, g_qn_a, g_kn_a, g_qn_b, g_kn_b, w_branch_a, w_branch_b, w_out, w_peer_q, peer_subkeys, peer_u, peer_v):
    depth = w_ada.shape[0]
    bsz, t, d = x_prompt.shape
    bs, ts, _ = x_sample.shape
    xp, xs = x_prompt, x_sample
    rows_p, rows_s = [], []
    n_c = bsz + bs
    c_all = jnp.pad(jnp.concatenate([c_prompt, c_sample], axis=0), ((0, (-n_c) % 8), (0, 0)))
    for l in range(depth):
        ada = _ada(c_all, w_ada[l], b_ada[l]).reshape(c_all.shape[0], N_ADA, d)
        w = _split_w_in(w_in[l], d)
        p = dict(g_norm1=g_norm1[l], g_norm2=g_norm2[l], b_fgate=b_fgate[l], g_qn_a=g_qn_a[l], g_kn_a=g_kn_a[l],
                 g_qn_b=g_qn_b[l], g_kn_b=g_kn_b[l],
                 w_branch_a=w_branch_a[l].astype(BF16), w_branch_b=w_branch_b[l].astype(BF16),
                 w_out=w_out[l].astype(BF16), wq_t=w_peer_q[l].T.astype(BF16),
                 subkeys=peer_subkeys[l].reshape(PEER_HEADS * 2, PEER_KEYS, -1).astype(BF16),
                 u16=peer_u[l].astype(BF16), vt16=peer_v[l].T.astype(BF16))
        caches = (cache_fox_k[l], cache_fox_v[l], cache_fox_logf[l], cache_dsa_k[l], cache_dsa_v[l], cache_idx_k[l])
        xp, f_p = _prompt_group(xp, ada[:bsz], w, p)
        xs, f_s = _sample_group(xs, ada[bsz:n_c], w, p, caches, page_table)
        rows_p.append(_layer_outputs(f_p, bsz, t))
        rows_s.append(_layer_outputs(f_s, bs, ts))
    outs_p = [jnp.stack([r[i] for r in rows_p], axis=0) for i in range(6)]
    outs_s = [jnp.stack([r[i] for r in rows_s], axis=0) for i in range(6)]
    return (xp, xs, *outs_p, *outs_s)
```

```python
import functools

import jax
import jax.numpy as jnp
from jax import lax
from jax.experimental import pallas as pl
from jax.experimental.pallas import tpu as pltpu

F32 = jnp.float32
BF16 = jnp.bfloat16
I32 = jnp.int32

HEAD_DIM = 128
A_HEADS = 8
A_KV_HEADS = 4
B_HEADS = 8
B_KV_HEADS = 4
IDX_HEADS = 16
IDX_DIM = 64
ROPE_THETA = 500000.0
ROPE_FRAC = 4
TOPK_MAX = 256
PEER_KEYS = 128
PEER_HEADS = 8
PEER_TOPK = 16
PAGE_SIZE = 128
EPS = 1e-6
ATTN_SCALE = HEAD_DIM ** -0.5
N_ADA = 6
LANES = 128
NEG = -1e30
LOWEST = -1.0e38
BISECT_ITERS = 40
VMEM_LIMIT = 56 * 1024 * 1024


def _cparams(*sem):
    return pltpu.CompilerParams(dimension_semantics=sem, vmem_limit_bytes=VMEM_LIMIT)


def _tile(n, pref, mult=LANES):
    best = None
    t = mult
    while t <= min(n, pref):
        if n % t == 0:
            best = t
        t += mult
    return best if best is not None else n


def _dot(a, b):
    return jnp.dot(a, b, preferred_element_type=F32)


def _dot_nt(a, b):
    return lax.dot_general(a, b, (((1,), (1,)), ((), ())), preferred_element_type=F32)


def _sigmoid(x):
    return 1.0 / (1.0 + jnp.exp(-x))


def _rep(n, pref):
    return max(r for r in range(1, min(n, pref) + 1) if n % r == 0)


def _ada_kernel(c_ref, w_ref, b_ref, o_ref):
    c = c_ref[...]
    s = c * _sigmoid(c)
    o_ref[...] = _dot(s.astype(BF16), w_ref[...].astype(BF16)) + b_ref[...]


def _ada(c_all, w_ada, b_ada):
    m, d = c_all.shape
    n = w_ada.shape[1]
    tn = _tile(n, 1024)
    return pl.pallas_call(
        _ada_kernel,
        grid=(n // tn,),
        in_specs=[pl.BlockSpec((m, d), lambda j: (0, 0)),
                  pl.BlockSpec((d, tn), lambda j: (0, j)),
                  pl.BlockSpec((1, tn), lambda j: (0, j))],
        out_specs=pl.BlockSpec((m, tn), lambda j: (0, j)),
        out_shape=jax.ShapeDtypeStruct((m, n), F32),
        compiler_params=_cparams("arbitrary"),
        name="ada",
    )(c_all, w_ada, b_ada.reshape(1, n))


def _rms_mod(x, g, sc, sh):
    y = x * lax.rsqrt(jnp.mean(x * x, axis=-1, keepdims=True) + EPS)
    return (y * g) * (1.0 + sc) + sh


def _prenorm_kernel(x_ref, g_ref, sc_ref, sh_ref, o_ref):
    o_ref[...] = _rms_mod(x_ref[...], g_ref[...], sc_ref[...], sh_ref[...]).astype(o_ref.dtype)


def _mod_spec(mod, tm, rows_per_group):
    r, d = mod.shape[1], mod.shape[2]
    return pl.BlockSpec((None, r, d), lambda i: ((i * tm) // rows_per_group, 0, 0))


def _prenorm(x, g, sc, sh, tm, rows_per_group):
    n, d = x.shape
    return pl.pallas_call(
        _prenorm_kernel,
        grid=(n // tm,),
        in_specs=[pl.BlockSpec((tm, d), lambda i: (i, 0)),
                  pl.BlockSpec((1, d), lambda i: (0, 0)),
                  _mod_spec(sc, tm, rows_per_group),
                  _mod_spec(sh, tm, rows_per_group)],
        out_specs=pl.BlockSpec((tm, d), lambda i: (i, 0)),
        out_shape=jax.ShapeDtypeStruct((n, d), BF16),
        compiler_params=_cparams("arbitrary"),
        name="prenorm",
    )(x, g.reshape(1, d), sc, sh)


def _rope_tables(pos, head_dim):
    rd = head_dim // ROPE_FRAC
    half = rd // 2
    inv = ROPE_THETA ** (-jnp.arange(half, dtype=F32) / half)
    ang = pos.astype(F32)[:, None] * inv[None, :]
    cos, sin = jnp.cos(ang), jnp.sin(ang)
    t = pos.shape[0]
    rest = head_dim - rd
    c = jnp.concatenate([cos, cos, jnp.ones((t, rest), F32)], axis=-1)
    sa = jnp.concatenate([-sin, jnp.zeros((t, half + rest), F32)], axis=-1)
    sb = jnp.concatenate([jnp.zeros((t, half), F32), sin, jnp.zeros((t, rest), F32)], axis=-1)
    reps = LANES // head_dim
    return tuple(jnp.tile(a, (1, reps)) for a in (c, sa, sb)), half


def _rope(y, c, sa, sb, half):
    return y * c + pltpu.roll(y, LANES - half, 1) * sa + pltpu.roll(y, half, 1) * sb


def _head_norm(chunk, gain):
    y = chunk * lax.rsqrt(jnp.mean(chunk * chunk, axis=-1, keepdims=True) + EPS)
    return y * gain


def _qkv_kernel(*refs, n_q, n_k, rope_half, with_bf16):
    h_ref, w_ref, gain_ref = refs[:3]
    pos = 3
    if rope_half:
        c_ref, sa_ref, sb_ref = refs[3:6]
        pos = 6
    q_ref, k_ref, v_ref = refs[pos:pos + 3]
    z = _dot(h_ref[...], w_ref[...])
    for ch in range(n_q + n_k):
        sl = slice(ch * LANES, (ch + 1) * LANES)
        y = _head_norm(z[:, sl], gain_ref[:, sl])
        if rope_half:
            y = _rope(y, c_ref[...], sa_ref[...], sb_ref[...], rope_half)
        if ch < n_q:
            q_ref[:, sl] = y.astype(q_ref.dtype)
        else:
            k_ref[:, (ch - n_q) * LANES:(ch - n_q + 1) * LANES] = y
    v = z[:, (n_q + n_k) * LANES:]
    v_ref[...] = v
    if with_bf16:
        kb_ref, vb_ref = refs[pos + 3:pos + 5]
        kb_ref[...] = k_ref[...].astype(BF16)
        vb_ref[...] = v.astype(BF16)


def _qkv_proj(h, w, gain, tables, rope_half, tm, n_q, n_k, with_bf16):
    n, d = h.shape
    wq, wk = n_q * LANES, n_k * LANES
    width = w.shape[1]
    row = lambda i: (i, 0)
    const = lambda i: (0, 0)
    in_specs = [pl.BlockSpec((tm, d), row), pl.BlockSpec((d, width), const),
                pl.BlockSpec((1, wq + wk), const)]
    args = [h, w, gain]
    if rope_half:
        nt = tables[0].shape[0] // tm
        for t in tables:
            in_specs.append(pl.BlockSpec((tm, LANES), lambda i: (i % nt, 0)))
            args.append(t)
    out_shape = [jax.ShapeDtypeStruct((n, wq), BF16), jax.ShapeDtypeStruct((n, wk), F32),
                 jax.ShapeDtypeStruct((n, wk), F32)]
    out_specs = [pl.BlockSpec((tm, wq), row), pl.BlockSpec((tm, wk), row), pl.BlockSpec((tm, wk), row)]
    if with_bf16:
        out_shape += [jax.ShapeDtypeStruct((n, wk), BF16)] * 2
        out_specs += [pl.BlockSpec((tm, wk), row)] * 2
    return pl.pallas_call(
        functools.partial(_qkv_kernel, n_q=n_q, n_k=n_k, rope_half=rope_half, with_bf16=with_bf16),
        grid=(n // tm,), in_specs=in_specs, out_specs=out_specs, out_shape=out_shape,
        compiler_params=_cparams("arbitrary"), name="qkv_proj",
    )(*args)


def _qidx_kernel(h_ref, w_ref, c_ref, sa_ref, sb_ref, o_ref, *, half):
    z = _dot(h_ref[...], w_ref[...])
    for ch in range(IDX_HEADS // 2):
        y = _rope(z[:, ch * LANES:(ch + 1) * LANES], c_ref[...], sa_ref[...], sb_ref[...], half)
        o_ref[2 * ch] = y[:, :IDX_DIM].astype(o_ref.dtype)
        o_ref[2 * ch + 1] = y[:, IDX_DIM:].astype(o_ref.dtype)


def _qidx_proj(h, w, tables, half, tm):
    n, d = h.shape
    nt = tables[0].shape[0] // tm
    tab = pl.BlockSpec((tm, LANES), lambda i: (i % nt, 0))
    return pl.pallas_call(
        functools.partial(_qidx_kernel, half=half),
        grid=(n // tm,),
        in_specs=[pl.BlockSpec((tm, d), lambda i: (i, 0)),
                  pl.BlockSpec((d, IDX_HEADS * IDX_DIM), lambda i: (0, 0)), tab, tab, tab],
        out_specs=pl.BlockSpec((IDX_HEADS, tm, IDX_DIM), lambda i: (0, i, 0)),
        out_shape=jax.ShapeDtypeStruct((IDX_HEADS, n, IDX_DIM), BF16),
        compiler_params=_cparams("arbitrary"), name="qidx_proj",
    )(h, w, *tables)


def _misc_kernel(h_ref, w_ref, bf_ref, c_ref, sa_ref, sb_ref, logf_ref, ki_ref, wi_ref, *, half):
    z = _dot(h_ref[...], w_ref[...])
    f = z[:, :A_HEADS] + bf_ref[...]
    logf_ref[...] = jnp.minimum(f, 0.0) - jnp.log1p(jnp.exp(-jnp.abs(f)))
    y = _rope(z[:, LANES:2 * LANES], c_ref[...], sa_ref[...], sb_ref[...], half)
    ki_ref[...] = y[:, :IDX_DIM]
    wi_ref[...] = z[:, 2 * LANES:2 * LANES + IDX_HEADS]


def _misc_proj(h, w, b_fgate, tables, half, tm):
    n, d = h.shape
    nt = tables[0].shape[0] // tm
    tab = pl.BlockSpec((tm, LANES), lambda i: (i % nt, 0))
    row = lambda i: (i, 0)
    return pl.pallas_call(
        functools.partial(_misc_kernel, half=half),
        grid=(n // tm,),
        in_specs=[pl.BlockSpec((tm, d), row), pl.BlockSpec((d, 3 * LANES), lambda i: (0, 0)),
                  pl.BlockSpec((1, A_HEADS), lambda i: (0, 0)), tab, tab, tab],
        out_specs=[pl.BlockSpec((tm, A_HEADS), row), pl.BlockSpec((tm, IDX_DIM), row),
                   pl.BlockSpec((tm, IDX_HEADS), row)],
        out_shape=[jax.ShapeDtypeStruct((n, A_HEADS), F32), jax.ShapeDtypeStruct((n, IDX_DIM), F32),
                   jax.ShapeDtypeStruct((n, IDX_HEADS), F32)],
        compiler_params=_cparams("arbitrary"), name="misc_proj",
    )(h, w, b_fgate.reshape(1, A_HEADS), *tables)


def _gate_kernel(h_ref, w_ref, o_ref):
    o_ref[...] = _sigmoid(_dot(h_ref[...], w_ref[...]))


def _gate_proj(h, w, tm):
    n, d = h.shape
    width = w.shape[1]
    return pl.pallas_call(
        _gate_kernel, grid=(n // tm,),
        in_specs=[pl.BlockSpec((tm, d), lambda i: (i, 0)), pl.BlockSpec((d, width), lambda i: (0, 0))],
        out_specs=pl.BlockSpec((tm, width), lambda i: (i, 0)),
        out_shape=jax.ShapeDtypeStruct((n, width), F32),
        compiler_params=_cparams("arbitrary"), name="gate_proj",
    )(h, w)


def _cumsum_kernel(x_ref, o_ref, *, tc):
    t = x_ref.shape[-1]
    r = lax.broadcasted_iota(I32, (tc, tc), 0)
    c = lax.broadcasted_iota(I32, (tc, tc), 1)
    tri = jnp.where(r <= c, 1.0, 0.0).astype(F32)
    carry = jnp.zeros((x_ref.shape[0], 1), F32)
    for blk in range(t // tc):
        sl = slice(blk * tc, (blk + 1) * tc)
        cs = jnp.dot(x_ref[:, sl], tri, precision=lax.Precision.HIGHEST, preferred_element_type=F32) + carry
        o_ref[:, sl] = cs
        carry = cs[:, tc - 1:tc]


def _cumsum_lanes(x):
    b, h, t = x.shape
    tc = _tile(t, 512)
    return pl.pallas_call(
        functools.partial(_cumsum_kernel, tc=tc), grid=(b,),
        in_specs=[pl.BlockSpec((None, h, t), lambda i: (i, 0, 0))],
        out_specs=pl.BlockSpec((None, h, t), lambda i: (i, 0, 0)),
        out_shape=jax.ShapeDtypeStruct((b, h, t), F32),
        compiler_params=_cparams("arbitrary"), name="cumsum",
    )(x)


def _online_update(s, v16, m_ref, l_ref, acc_ref):
    m_prev = m_ref[...]
    m_new = jnp.maximum(m_prev, jnp.max(s, axis=-1, keepdims=True))
    alpha = jnp.exp(m_prev - m_new)
    p = jnp.exp(s - m_new)
    l_ref[...] = alpha * l_ref[...] + jnp.sum(p, axis=-1, keepdims=True)
    acc_ref[...] = alpha * acc_ref[...] + _dot(p.astype(BF16), v16)
    m_ref[...] = m_new


def _fox_kernel(q_ref, k_ref, v_ref, fq_ref, fk_ref, o_ref, m_sc, l_sc, acc_sc, *, tq, tk):
    i = pl.program_id(2)
    kk = pl.program_id(3)
    g_heads = q_ref.shape[1] // LANES

    @pl.when(kk == 0)
    def _():
        m_sc[...] = jnp.full(m_sc.shape, NEG, F32)
        l_sc[...] = jnp.zeros(l_sc.shape, F32)
        acc_sc[...] = jnp.zeros(acc_sc.shape, F32)

    @pl.when(kk <= i)
    def _():
        k16 = k_ref[...].astype(BF16)
        v16 = v_ref[...].astype(BF16)
        row = i * tq + lax.broadcasted_iota(I32, (tq, tk), 0)
        col = kk * tk + lax.broadcasted_iota(I32, (tq, tk), 1)
        causal = col <= row
        for g in range(g_heads):
            s = _dot_nt(q_ref[:, g * LANES:(g + 1) * LANES], k16) * ATTN_SCALE
            s = s + fq_ref[:, g:g + 1] - fk_ref[g:g + 1, :]
            s = jnp.where(causal, s, NEG)
            _online_update(s, v16, m_sc.at[g], l_sc.at[g], acc_sc.at[g])

    @pl.when(kk == pl.num_programs(3) - 1)
    def _():
        for g in range(g_heads):
            o_ref[:, g * LANES:(g + 1) * LANES] = (acc_sc[g] / l_sc[g]).astype(o_ref.dtype)


def _fox_prompt(q, k, v, fq, fk, bsz, t):
    g = A_HEADS // A_KV_HEADS
    tq = tk = _tile(t, 512)
    nq = t // tq
    return pl.pallas_call(
        functools.partial(_fox_kernel, tq=tq, tk=tk),
        grid=(bsz, A_KV_HEADS, nq, nq),
        in_specs=[pl.BlockSpec((tq, g * LANES), lambda b, j, i, kk: (b * nq + i, j)),
                  pl.BlockSpec((tk, LANES), lambda b, j, i, kk: (b * nq + jnp.minimum(kk, i), j)),
                  pl.BlockSpec((tk, LANES), lambda b, j, i, kk: (b * nq + jnp.minimum(kk, i), j)),
                  pl.BlockSpec((None, None, tq, g), lambda b, j, i, kk: (b, j, i, 0)),
                  pl.BlockSpec((None, None, g, tk), lambda b, j, i, kk: (b, j, 0, jnp.minimum(kk, i)))],
        out_specs=pl.BlockSpec((tq, g * LANES), lambda b, j, i, kk: (b * nq + i, j)),
        out_shape=jax.ShapeDtypeStruct((bsz * t, A_HEADS * LANES), BF16),
        scratch_shapes=[pltpu.VMEM((g, tq, 1), F32), pltpu.VMEM((g, tq, 1), F32),
                        pltpu.VMEM((g, tq, LANES), F32)],
        compiler_params=_cparams("arbitrary", "arbitrary", "arbitrary", "arbitrary"),
        name="fox_prompt",
    )(q, k, v, fq, fk)


def _kth_value(count_ge, lo, hi, c_lo, k):
    def cond(st):
        return jnp.logical_and(st[0] < BISECT_ITERS, jnp.max(st[3]) > k)

    def body(st):
        it, lo, hi, c_lo = st
        mid = 0.5 * lo + 0.5 * hi
        c = count_ge(mid)
        ge = c >= k
        return it + 1, jnp.where(ge, mid, lo), jnp.where(ge, hi, mid), jnp.where(ge, c, c_lo)

    return lax.while_loop(cond, body, (jnp.int32(0), lo, hi, c_lo))[1]


def _dsa_kernel(qb_ref, qi_ref, wi_ref, ki_ref, kb_ref, vb_ref, o_ref, key_sc, m_sc, l_sc, acc_sc,
                *, tq, topk):
    i = pl.program_id(1)
    nch = i + 1
    row = i * tq + lax.broadcasted_iota(I32, (tq, 1), 0)

    def scores(c, carry):
        mx, mn = carry
        kc = ki_ref[pl.ds(pl.multiple_of(c * tq, tq), tq), :].astype(BF16)
        acc = jnp.zeros((tq, tq), F32)
        for h in range(IDX_HEADS):
            acc = acc + wi_ref[:, h:h + 1] * jnp.maximum(_dot_nt(qi_ref[h], kc), 0.0)
        valid = (c * tq + lax.broadcasted_iota(I32, (1, tq), 1)) <= row
        key_sc[c] = jnp.where(valid, acc, -jnp.inf)
        mx = jnp.maximum(mx, jnp.max(jnp.where(valid, acc, -jnp.inf), axis=-1, keepdims=True))
        mn = jnp.minimum(mn, jnp.min(jnp.where(valid, acc, jnp.inf), axis=-1, keepdims=True))
        return mx, mn

    mx, mn = lax.fori_loop(0, nch, scores, (jnp.full((tq, 1), -jnp.inf, F32), jnp.full((tq, 1), jnp.inf, F32)))

    def count_ge(cand):
        def cbody(c, cnt):
            ge = jnp.where(key_sc[c] >= cand, 1.0, 0.0)
            for s in range(tq // LANES):
                cnt = cnt + ge[:, s * LANES:(s + 1) * LANES]
            return cnt
        cnt = lax.fori_loop(0, nch, cbody, jnp.zeros((tq, LANES), F32))
        return jnp.sum(cnt, axis=-1, keepdims=True)

    many = row >= topk
    lo0 = jnp.where(many, mn, LOWEST)
    hi0 = jnp.where(many, mx, LOWEST)
    c0 = jnp.where(many, (row + 1).astype(F32), 0.0)
    thr = _kth_value(count_ge, lo0, hi0, c0, float(topk))

    m_sc[...] = jnp.full(m_sc.shape, NEG, F32)
    l_sc[...] = jnp.zeros(l_sc.shape, F32)
    acc_sc[...] = jnp.zeros(acc_sc.shape, F32)

    def attend(c, carry):
        bias = jnp.where(key_sc[c] >= thr, 0.0, NEG)
        rows = pl.ds(pl.multiple_of(c * tq, tq), tq)
        for h in range(B_HEADS):
            j = h // (B_HEADS // B_KV_HEADS)
            k16 = kb_ref[rows, j * LANES:(j + 1) * LANES]
            v16 = vb_ref[rows, j * LANES:(j + 1) * LANES]
            s = _dot_nt(qb_ref[:, h * LANES:(h + 1) * LANES], k16) * ATTN_SCALE + bias
            _online_update(s, v16, m_sc.at[h], l_sc.at[h], acc_sc.at[h])
        return carry

    lax.fori_loop(0, nch, attend, 0)
    for h in range(B_HEADS):
        o_ref[:, h * LANES:(h + 1) * LANES] = (acc_sc[h] / l_sc[h]).astype(o_ref.dtype)


def _dsa_prompt(qb, qi, wi, ki, kb16, vb16, bsz, t):
    tq = _tile(t, 256)
    nq = t // tq
    topk = min(TOPK_MAX, t // 4)
    kvw = B_KV_HEADS * LANES
    return pl.pallas_call(
        functools.partial(_dsa_kernel, tq=tq, topk=topk),
        grid=(bsz, nq),
        in_specs=[pl.BlockSpec((tq, B_HEADS * LANES), lambda b, i: (b * nq + i, 0)),
                  pl.BlockSpec((IDX_HEADS, tq, IDX_DIM), lambda b, i: (0, b * nq + i, 0)),
                  pl.BlockSpec((tq, IDX_HEADS), lambda b, i: (b * nq + i, 0)),
                  pl.BlockSpec((t, IDX_DIM), lambda b, i: (b, 0)),
                  pl.BlockSpec((t, kvw), lambda b, i: (b, 0)),
                  pl.BlockSpec((t, kvw), lambda b, i: (b, 0))],
        out_specs=pl.BlockSpec((tq, B_HEADS * LANES), lambda b, i: (b * nq + i, 0)),
        out_shape=jax.ShapeDtypeStruct((bsz * t, B_HEADS * LANES), BF16),
        scratch_shapes=[pltpu.VMEM((nq, tq, tq), F32),
                        pltpu.VMEM((B_HEADS, tq, 1), F32), pltpu.VMEM((B_HEADS, tq, 1), F32),
                        pltpu.VMEM((B_HEADS, tq, LANES), F32)],
        compiler_params=_cparams("arbitrary", "arbitrary"),
        name="dsa_prompt",
    )(qb, qi, wi, ki, kb16, vb16)


def _page_specs(block, n_rep):
    def spec(r):
        return pl.BlockSpec(block, lambda b, pg, pt: (pt[b, pg * n_rep + r],) + (0,) * (len(block) - 1))
    return [spec(r) for r in range(n_rep)]


def _fscan_kernel(pt_ref, *refs, n_rep):
    pages = refs[:n_rep]
    new_ref, fp_ref, fn_ref, carry_sc = refs[n_rep:]
    pg = pl.program_id(1)
    r = lax.broadcasted_iota(I32, (PAGE_SIZE, PAGE_SIZE), 0)
    c = lax.broadcasted_iota(I32, (PAGE_SIZE, PAGE_SIZE), 1)
    tri = jnp.where(r <= c, 1.0, 0.0).astype(F32)

    def cs_of(x):
        return jnp.dot(x, tri, precision=lax.Precision.HIGHEST, preferred_element_type=F32)

    @pl.when(pg == 0)
    def _():
        carry_sc[...] = jnp.zeros(carry_sc.shape, F32)

    carry = carry_sc[...]
    for rr in range(n_rep):
        cs = cs_of(pages[rr][...])
        fp_ref[:, rr * PAGE_SIZE:(rr + 1) * PAGE_SIZE] = cs + carry
        carry = carry + cs[:, PAGE_SIZE - 1:PAGE_SIZE]
    carry_sc[...] = carry

    @pl.when(pg == pl.num_programs(1) - 1)
    def _():
        fn_ref[...] = cs_of(new_ref[...]) + carry


def _fscan(page_table, logf_pool_t, logf_new_t, n_rep):
    bsz, n_pages = page_table.shape
    h = logf_pool_t.shape[1]
    grid_spec = pltpu.PrefetchScalarGridSpec(
        num_scalar_prefetch=1, grid=(bsz, n_pages // n_rep),
        in_specs=_page_specs((None, h, PAGE_SIZE), n_rep)
        + [pl.BlockSpec((None, h, PAGE_SIZE), lambda b, pg, pt: (b, 0, 0))],
        out_specs=[pl.BlockSpec((None, h, n_rep * PAGE_SIZE), lambda b, pg, pt: (b, 0, pg)),
                   pl.BlockSpec((None, h, PAGE_SIZE), lambda b, pg, pt: (b, 0, 0))],
        scratch_shapes=[pltpu.VMEM((h, 1), F32)])
    return pl.pallas_call(
        functools.partial(_fscan_kernel, n_rep=n_rep), grid_spec=grid_spec,
        out_shape=[jax.ShapeDtypeStruct((bsz, h, n_pages * PAGE_SIZE), F32),
                   jax.ShapeDtypeStruct((bsz, h, PAGE_SIZE), F32)],
        compiler_params=_cparams("arbitrary", "arbitrary"), name="fscan",
    )(page_table, *([logf_pool_t] * n_rep), logf_new_t)


def _sidx_kernel(pt_ref, *refs, n_rep, n_tok, topk):
    pages = refs[:n_rep]
    qi_ref, wi_ref, kn_ref, kb_ref, nb_ref, sc_sc = refs[n_rep:]
    pg = pl.program_id(1)
    npg = pl.num_programs(1)
    rows = 8
    pad = jnp.zeros((rows - n_tok, PAGE_SIZE), F32)

    def score(keys_f32):
        d = jnp.maximum(_dot_nt(qi_ref[...], keys_f32.astype(BF16)), 0.0) * wi_ref[...]
        return jnp.concatenate([jnp.sum(d.reshape(n_tok, IDX_HEADS, PAGE_SIZE), axis=1), pad], axis=0)

    sc_sc[pg] = jnp.concatenate([score(pages[rr][...]) for rr in range(n_rep)], axis=1)

    @pl.when(pg == npg - 1)
    def _():
        t_idx = lax.broadcasted_iota(I32, (rows, PAGE_SIZE), 0)
        s_idx = lax.broadcasted_iota(I32, (rows, PAGE_SIZE), 1)
        new_valid = (s_idx <= t_idx) & (t_idx < n_tok)
        new_raw = score(kn_ref[...])
        new = jnp.where(new_valid, new_raw, -jnp.inf)
        past = sc_sc[...]

        def count_ge(cand):
            c_past = jnp.sum(jnp.sum(jnp.where(past >= cand[None], 1.0, 0.0), axis=0), axis=-1, keepdims=True)
            return c_past + jnp.sum(jnp.where(new >= cand, 1.0, 0.0), axis=-1, keepdims=True)

        mx = jnp.maximum(jnp.max(jnp.max(past, axis=0), axis=-1, keepdims=True),
                         jnp.max(new, axis=-1, keepdims=True))
        mn = jnp.minimum(jnp.min(jnp.min(past, axis=0), axis=-1, keepdims=True),
                         jnp.min(jnp.where(new_valid, new_raw, jnp.inf), axis=-1, keepdims=True))
        tok = lax.broadcasted_iota(I32, (rows, 1), 0)
        n_valid = jnp.where(tok < n_tok, (past.shape[0] * past.shape[2] + 1 + tok).astype(F32), 0.0)
        many = n_valid > topk
        thr = _kth_value(count_ge, jnp.where(many, mn, LOWEST), jnp.where(many, mx, LOWEST),
                         jnp.where(many, n_valid, 0.0), float(topk))
        kb_ref[...] = jnp.where(past >= thr[None], 0.0, NEG)
        nb_ref[...] = jnp.where(new >= thr, 0.0, NEG)


def _sample_index(page_table, idx_pool, qi, wi, ki_new, n_rep, n_tok):
    bsz, n_pages = page_table.shape
    p = n_pages * PAGE_SIZE
    topk = min(TOPK_MAX, (p + n_tok) // 4)
    nr = n_tok * IDX_HEADS
    steps = n_pages // n_rep
    wide = n_rep * PAGE_SIZE
    grid_spec = pltpu.PrefetchScalarGridSpec(
        num_scalar_prefetch=1, grid=(bsz, steps),
        in_specs=_page_specs((None, PAGE_SIZE, IDX_DIM), n_rep)
        + [pl.BlockSpec((None, nr, IDX_DIM), lambda b, pg, pt: (b, 0, 0)),
           pl.BlockSpec((None, nr, 1), lambda b, pg, pt: (b, 0, 0)),
           pl.BlockSpec((None, PAGE_SIZE, IDX_DIM), lambda b, pg, pt: (b, 0, 0))],
        out_specs=[pl.BlockSpec((None, steps, 8, wide), lambda b, pg, pt: (b, 0, 0, 0)),
                   pl.BlockSpec((None, 8, PAGE_SIZE), lambda b, pg, pt: (b, 0, 0))],
        scratch_shapes=[pltpu.VMEM((steps, 8, wide), F32)])
    return pl.pallas_call(
        functools.partial(_sidx_kernel, n_rep=n_rep, n_tok=n_tok, topk=topk), grid_spec=grid_spec,
        out_shape=[jax.ShapeDtypeStruct((bsz, steps, 8, wide), F32),
                   jax.ShapeDtypeStruct((bsz, 8, PAGE_SIZE), F32)],
        compiler_params=_cparams("arbitrary", "arbitrary"), name="sample_index",
    )(page_table, *([idx_pool] * n_rep), qi, wi, ki_new)


def _paged_attn_kernel(pt_ref, *refs, n_rep, n_kv, n_heads, per_head):
    kp = refs[:n_rep]
    vp = refs[n_rep:2 * n_rep]
    q_ref, kb_ref, rb_ref, kn_ref, vn_ref, nb_ref, o_ref, m_sc, l_sc, acc_sc = refs[2 * n_rep:]
    pg = pl.program_id(1)
    nrow = q_ref.shape[0]
    wide = PAGE_SIZE * n_kv

    @pl.when(pg == 0)
    def _():
        m_sc[...] = jnp.full(m_sc.shape, NEG, F32)
        l_sc[...] = jnp.zeros(l_sc.shape, F32)
        acc_sc[...] = jnp.zeros(acc_sc.shape, F32)

    def expand(kb):
        if per_head:
            return jnp.concatenate([kb] * (nrow // kb.shape[0]), axis=0)
        return jnp.concatenate([jnp.broadcast_to(kb[t:t + 1], (8, kb.shape[1])) for t in range(nrow // 8)], axis=0)

    head = lax.broadcasted_iota(I32, (nrow, wide), 0) % n_heads
    kv_col = lax.broadcasted_iota(I32, (nrow, wide), 1) % n_kv
    head_bias = jnp.where(head // (n_heads // n_kv) == kv_col, 0.0, NEG)
    q = q_ref[...]
    rb = rb_ref[...]

    def update(s, vs, width):
        m_prev = m_sc[...]
        m_new = jnp.maximum(m_prev, jnp.max(s, axis=-1, keepdims=True))
        alpha = jnp.exp(m_prev - m_new)
        p = jnp.exp(s - m_new)
        l_sc[...] = alpha * l_sc[...] + jnp.sum(p, axis=-1, keepdims=True)
        acc = alpha * acc_sc[...]
        for r, v in enumerate(vs):
            acc = acc + _dot(p[:, r * width:(r + 1) * width].astype(BF16), v.astype(BF16))
        acc_sc[...] = acc
        m_sc[...] = m_new

    s = jnp.concatenate([_dot_nt(q, kp[r][...].astype(BF16)) for r in range(n_rep)], axis=1)
    bias = expand(kb_ref[...]) + jnp.concatenate([head_bias] * n_rep, axis=1)
    update(s * ATTN_SCALE + rb + bias, [vp[r][...] for r in range(n_rep)], wide)

    @pl.when(pg == pl.num_programs(1) - 1)
    def _():
        sn = _dot_nt(q, kn_ref[...].astype(BF16)) * ATTN_SCALE + rb + nb_ref[...]
        update(sn, [vn_ref[...]], PAGE_SIZE)
        o_ref[...] = acc_sc[...] / l_sc[...]


def _paged_attn(page_table, k_pool, v_pool, q, key_bias, row_bias, k_new, v_new, new_bias, n_rep, n_kv, n_heads,
                per_head):
    bsz, n_pages = page_table.shape
    nrow, w = q.shape[1], q.shape[2]
    kbh = key_bias.shape[1]
    wide = PAGE_SIZE * n_kv
    const = lambda b, pg, pt: (b, 0, 0)
    grid_spec = pltpu.PrefetchScalarGridSpec(
        num_scalar_prefetch=1, grid=(bsz, n_pages // n_rep),
        in_specs=_page_specs((wide, w), n_rep) + _page_specs((wide, w), n_rep)
        + [pl.BlockSpec((None, nrow, w), const),
           pl.BlockSpec((None, kbh, n_rep * wide), lambda b, pg, pt: (b, 0, pg)),
           pl.BlockSpec((None, nrow, 1), const),
           pl.BlockSpec((None, PAGE_SIZE, w), const),
           pl.BlockSpec((None, PAGE_SIZE, w), const),
           pl.BlockSpec((None, nrow, PAGE_SIZE), const)],
        out_specs=pl.BlockSpec((None, nrow, w), const),
        scratch_shapes=[pltpu.VMEM((nrow, 1), F32), pltpu.VMEM((nrow, 1), F32), pltpu.VMEM((nrow, w), F32)])
    return pl.pallas_call(
        functools.partial(_paged_attn_kernel, n_rep=n_rep, n_kv=n_kv, n_heads=n_heads, per_head=per_head),
        grid_spec=grid_spec,
        out_shape=jax.ShapeDtypeStruct((bsz, nrow, w), F32),
        compiler_params=_cparams("arbitrary", "arbitrary"), name="paged_attn",
    )(page_table, *([k_pool] * n_rep), *([v_pool] * n_rep), q, key_bias, row_bias, k_new, v_new, new_bias)


def _merge_kernel(oa_ref, ob_ref, ga_ref, gb_ref, wa_ref, wb_ref, o_ref):
    m = ga_ref[...] * _dot(oa_ref[...], wa_ref[...]) + gb_ref[...] * _dot(ob_ref[...], wb_ref[...])
    o_ref[...] = m.astype(o_ref.dtype)


def _merge(oa, ob, ga, gb, wa, wb, tm):
    n, e = oa.shape
    d = wa.shape[1]
    row = lambda i: (i, 0)
    const = lambda i: (0, 0)
    return pl.pallas_call(
        _merge_kernel, grid=(n // tm,),
        in_specs=[pl.BlockSpec((tm, e), row), pl.BlockSpec((tm, e), row),
                  pl.BlockSpec((tm, d), row), pl.BlockSpec((tm, d), row),
                  pl.BlockSpec((e, d), const), pl.BlockSpec((e, d), const)],
        out_specs=pl.BlockSpec((tm, d), row),
        out_shape=jax.ShapeDtypeStruct((n, d), BF16),
        compiler_params=_cparams("arbitrary"), name="merge",
    )(oa, ob, ga, gb, wa, wb)


def _outproj_kernel(x_ref, m_ref, w_ref, g1_ref, gn_ref, sc_ref, sh_ref, x1_ref, h2_ref):
    x1 = x_ref[...] + g1_ref[...] * _dot(m_ref[...], w_ref[...])
    x1_ref[...] = x1
    h2_ref[...] = _rms_mod(x1, gn_ref[...], sc_ref[...], sh_ref[...]).astype(h2_ref.dtype)


def _outproj(x, merged, w_out, ga1, g_norm2, sc2, sh2, tm, rows_per_group):
    n, d = x.shape
    row = lambda i: (i, 0)
    return pl.pallas_call(
        _outproj_kernel, grid=(n // tm,),
        in_specs=[pl.BlockSpec((tm, d), row), pl.BlockSpec((tm, d), row),
                  pl.BlockSpec((d, d), lambda i: (0, 0)),
                  _mod_spec(ga1, tm, rows_per_group),
                  pl.BlockSpec((1, d), lambda i: (0, 0)),
                  _mod_spec(sc2, tm, rows_per_group), _mod_spec(sh2, tm, rows_per_group)],
        out_specs=[pl.BlockSpec((tm, d), row), pl.BlockSpec((tm, d), row)],
        out_shape=[jax.ShapeDtypeStruct((n, d), F32), jax.ShapeDtypeStruct((n, d), BF16)],
        compiler_params=_cparams("arbitrary"), name="outproj",
    )(x, merged, w_out, ga1, g_norm2.reshape(1, d), sc2, sh2)


def _top_values(x, n):
    out = []
    for _ in range(n):
        m = jnp.max(x, axis=0, keepdims=True)
        out.append(m)
        x = jnp.where(x == m, -jnp.inf, x)
    return out


def _peer_query_kernel(ht_ref, wq_ref, sk_ref, s_ref, thr_ref, m1_ref, m2_ref, rz_ref, sv_sc):
    qt = _dot(wq_ref[...], ht_ref[...])
    n_hc = sk_ref.shape[0]
    for hc in range(n_hc):
        s = _dot(sk_ref[hc], qt[hc * PEER_KEYS:(hc + 1) * PEER_KEYS, :].astype(BF16))
        s_ref[hc] = s
        for r, m in enumerate(_top_values(s, PEER_TOPK)):
            sv_sc[hc, r:r + 1, :] = m
    for h in range(n_hc // 2):
        sv1 = sv_sc[2 * h]
        sv2 = sv_sc[2 * h + 1]
        cands = [sv1[0:1] + sv2]
        cands += [sv1[a:a + 1] + sv2[0:8] for a in range(1, 8)]
        cands += [sv1[8:16] + sv2[0:1]]
        tops = _top_values(jnp.concatenate(cands, axis=0), PEER_TOPK)
        z = jnp.zeros_like(tops[0])
        for m in tops:
            z = z + jnp.exp(m - tops[0])
        thr_ref[h:h + 1, :] = tops[-1]
        m1_ref[h:h + 1, :] = sv1[0:1]
        m2_ref[h:h + 1, :] = sv2[0:1]
        rz_ref[h:h + 1, :] = 1.0 / z


def _peer_query(h2t, wqt, subkeys, tm):
    d, n = h2t.shape
    n_hc = subkeys.shape[0]
    col = lambda i: (0, i)
    small = jax.ShapeDtypeStruct((PEER_HEADS, n), F32)
    small_spec = pl.BlockSpec((PEER_HEADS, tm), col)
    return pl.pallas_call(
        _peer_query_kernel, grid=(n // tm,),
        in_specs=[pl.BlockSpec((d, tm), col), pl.BlockSpec(wqt.shape, lambda i: (0, 0)),
                  pl.BlockSpec(subkeys.shape, lambda i: (0, 0, 0))],
        out_specs=[pl.BlockSpec((n_hc, PEER_KEYS, tm), lambda i: (0, 0, i))] + [small_spec] * 4,
        out_shape=[jax.ShapeDtypeStruct((n_hc, PEER_KEYS, n), F32)] + [small] * 4,
        scratch_shapes=[pltpu.VMEM((n_hc, PEER_TOPK, tm), F32)],
        compiler_params=_cparams("arbitrary"), name="peer_query",
    )(h2t, wqt, subkeys)


def _peer_dense_kernel(ht_ref, u_ref, vt_ref, s_ref, thr_ref, m1_ref, m2_ref, rz_ref, o_ref, e2_sc, *, ci1):
    e = pl.program_id(1)

    @pl.when(e == 0)
    def _():
        o_ref[...] = jnp.zeros(o_ref.shape, F32)
        for h in range(PEER_HEADS):
            e2_sc[h] = jnp.exp(s_ref[2 * h + 1] - m2_ref[h:h + 1, :]) * rz_ref[h:h + 1, :]

    a = _dot(u_ref[...], ht_ref[...])
    a = 0.5 * a * (1.0 + lax.erf(a * (2.0 ** -0.5)))
    blocks = []
    for q in range(ci1):
        i1 = e * ci1 + q
        w = jnp.zeros((PEER_KEYS, a.shape[1]), F32)
        for h in range(PEER_HEADS):
            s1 = s_ref[2 * h, pl.ds(i1, 1), :]
            sel = (s_ref[2 * h + 1] + s1) >= thr_ref[h:h + 1, :]
            w = w + jnp.where(sel, e2_sc[h] * jnp.exp(s1 - m1_ref[h:h + 1, :]), 0.0)
        blocks.append((w * a[q * PEER_KEYS:(q + 1) * PEER_KEYS, :]).astype(BF16))
    o_ref[...] += _dot(vt_ref[...], jnp.concatenate(blocks, axis=0))


def _peer_dense(h2t, u16, vt16, s_t, thr, m1, m2, rz, tm, te):
    d, n = h2t.shape
    n_exp = u16.shape[0]
    n_hc = s_t.shape[0]
    tok = lambda i, e: (0, i)
    small_spec = pl.BlockSpec((PEER_HEADS, tm), tok)
    return pl.pallas_call(
        functools.partial(_peer_dense_kernel, ci1=te // PEER_KEYS),
        grid=(n // tm, n_exp // te),
        in_specs=[pl.BlockSpec((d, tm), tok),
                  pl.BlockSpec((te, d), lambda i, e: (e, 0)),
                  pl.BlockSpec((d, te), lambda i, e: (0, e)),
                  pl.BlockSpec((n_hc, PEER_KEYS, tm), lambda i, e: (0, 0, i))] + [small_spec] * 4,
        out_specs=pl.BlockSpec((d, tm), tok),
        out_shape=jax.ShapeDtypeStruct((d, n), F32),
        scratch_shapes=[pltpu.VMEM((PEER_HEADS, PEER_KEYS, tm), F32)],
        compiler_params=_cparams("arbitrary", "arbitrary"), name="peer_dense",
    )(h2t, u16, vt16, s_t, thr, m1, m2, rz)


def _final_kernel(x_ref, pt_ref, g2_ref, o_ref):
    o_ref[...] = x_ref[...] + g2_ref[...] * pt_ref[...].T


def _final(x1, peer_t, ga2, tm, rows_per_group):
    n, d = x1.shape
    return pl.pallas_call(
        _final_kernel, grid=(n // tm,),
        in_specs=[pl.BlockSpec((tm, d), lambda i: (i, 0)), pl.BlockSpec((d, tm), lambda i: (0, i)),
                  _mod_spec(ga2, tm, rows_per_group)],
        out_specs=pl.BlockSpec((tm, d), lambda i: (i, 0)),
        out_shape=jax.ShapeDtypeStruct((n, d), F32),
        compiler_params=_cparams("arbitrary"), name="final",
    )(x1, peer_t, ga2)


def _split_w_in(w_in, d_model):
    sizes = (A_HEADS * HEAD_DIM, A_KV_HEADS * HEAD_DIM, A_KV_HEADS * HEAD_DIM, A_HEADS,
             B_HEADS * HEAD_DIM, B_KV_HEADS * HEAD_DIM, B_KV_HEADS * HEAD_DIM,
             IDX_HEADS * IDX_DIM, IDX_DIM, IDX_HEADS, d_model, d_model)
    offs = [0]
    for s in sizes:
        offs.append(offs[-1] + s)
    cols = lambda a, b: w_in[:, offs[a]:offs[b]].astype(BF16)
    padw = lambda a: jnp.pad(w_in[:, offs[a]:offs[a + 1]], ((0, 0), (0, LANES - sizes[a]))).astype(BF16)
    return dict(a=cols(0, 3), b=cols(4, 7), qi=cols(7, 8),
                misc=jnp.concatenate([padw(3), padw(8), padw(9)], axis=1),
                gate_a=cols(10, 11), gate_b=cols(11, 12))


def _front(x2, mods, pos, w, p, tm, rows_per_group):
    sh1, sc1 = mods[0], mods[1]
    h = _prenorm(x2, p["g_norm1"], sc1, sh1, tm, rows_per_group)
    tab128, half128 = _rope_tables(pos, HEAD_DIM)
    tab64, half64 = _rope_tables(pos, IDX_DIM)
    gain_a = jnp.concatenate([jnp.tile(p["g_qn_a"], A_HEADS), jnp.tile(p["g_kn_a"], A_KV_HEADS)]).reshape(1, -1)
    gain_b = jnp.concatenate([jnp.tile(p["g_qn_b"], B_HEADS), jnp.tile(p["g_kn_b"], B_KV_HEADS)]).reshape(1, -1)
    q_a, k_a, v_a = _qkv_proj(h, w["a"], gain_a, None, 0, tm, A_HEADS, A_KV_HEADS, False)
    q_b, k_b, v_b, kb16, vb16 = _qkv_proj(h, w["b"], gain_b, tab128, half128, tm, B_HEADS, B_KV_HEADS, True)
    q_i = _qidx_proj(h, w["qi"], tab64, half64, tm)
    logf, k_i, w_i = _misc_proj(h, w["misc"], p["b_fgate"], tab64, half64, tm)
    gate_a = _gate_proj(h, w["gate_a"], tm)
    gate_b = _gate_proj(h, w["gate_b"], tm)
    return dict(q_a=q_a, k_a=k_a, v_a=v_a, logf=logf, q_b=q_b, k_b=k_b, v_b=v_b, kb16=kb16, vb16=vb16,
                q_i=q_i, k_i=k_i, w_i=w_i, gate_a=gate_a, gate_b=gate_b)


def _back(x2, f, o_a, o_b, mods, p, tm, rows_per_group, tm_peer, te):
    ga1, sh2, sc2, ga2 = mods[2], mods[3], mods[4], mods[5]
    merged = _merge(o_a, o_b, f["gate_a"], f["gate_b"], p["w_branch_a"], p["w_branch_b"], tm)
    x1, h2 = _outproj(x2, merged, p["w_out"], ga1, p["g_norm2"], sc2, sh2, tm, rows_per_group)
    h2t = h2.T
    s_t, thr, m1, m2, rz = _peer_query(h2t, p["wq_t"], p["subkeys"], tm_peer)
    peer_t = _peer_dense(h2t, p["u16"], p["vt16"], s_t, thr, m1, m2, rz, tm_peer, te)
    return _final(x1, peer_t, ga2, tm, rows_per_group)


def _prompt_group(x, ada, w, p):
    bsz, t, d = x.shape
    tm = _tile(t, 512)
    mods = [ada[:, i].reshape(bsz, 1, d) for i in range(N_ADA)]
    x2 = x.reshape(bsz * t, d)
    f = _front(x2, mods, jnp.arange(t), w, p, tm, t)
    g = A_HEADS // A_KV_HEADS
    f_t = _cumsum_lanes(f["logf"].reshape(bsz, t, A_HEADS).transpose(0, 2, 1))
    fk = f_t.reshape(bsz, A_KV_HEADS, g, t)
    fq = fk.transpose(0, 1, 3, 2)
    o_a = _fox_prompt(f["q_a"], f["k_a"], f["v_a"], fq, fk, bsz, t)
    o_b = _dsa_prompt(f["q_b"], f["q_i"], f["w_i"], f["k_i"], f["kb16"], f["vb16"], bsz, t)
    y = _back(x2, f, o_a, o_b, mods, p, tm, t, _tile(bsz * t, 512), 512)
    return y.reshape(bsz, t, d), f


def _new_key_bias(base, n_heads, n_kv):
    bsz, n_tok = base.shape[0], base.shape[1]
    head_ok = (jnp.arange(n_heads)[:, None] // (n_heads // n_kv)) == jnp.arange(n_kv)[None, :]
    full = jnp.where(head_ok[None, None, :, None, :], base[..., None], NEG)
    full = full.reshape(bsz, n_tok * n_heads, n_tok * n_kv)
    return jnp.pad(full, ((0, 0), (0, 0), (0, PAGE_SIZE - n_tok * n_kv)), constant_values=NEG)


def _pad_rows(a, rows):
    return jnp.pad(a, ((0, 0), (0, rows - a.shape[1]), (0, 0)))


def _sample_group(x, ada, w, p, caches, page_table):
    bsz, n_tok, d = x.shape
    n = bsz * n_tok
    ck_a, cv_a, clf, ck_b, cv_b, cki = caches
    n_pages = page_table.shape[1]
    past = n_pages * PAGE_SIZE
    tm = _tile(n, 512)
    mods = [jnp.repeat(ada[:, i], n_tok, axis=0).reshape(n // tm, tm, d) for i in range(N_ADA)]
    x2 = x.reshape(n, d)
    pos = jnp.tile(past + jnp.arange(n_tok), bsz)
    f = _front(x2, mods, pos, w, p, tm, tm)
    rep_attn, rep_small = _rep(n_pages, 16), _rep(n_pages, 32)
    t_idx = jnp.arange(n_tok)
    causal = t_idx[None, :] <= t_idx[:, None]

    def rows128(a, n_kv):
        return _pad_rows(a.reshape(bsz, n_tok * n_kv, HEAD_DIM), PAGE_SIZE)

    lf_new = jnp.pad(f["logf"].reshape(bsz, n_tok, A_HEADS).transpose(0, 2, 1),
                     ((0, 0), (0, 0), (0, PAGE_SIZE - n_tok)))
    fp_t, fn_t = _fscan(page_table, clf, lf_new, rep_small)
    fn = fn_t[:, :, :n_tok].transpose(0, 2, 1)
    base = jnp.where(causal[None, :, None, :], -fn.transpose(0, 2, 1)[:, None, :, :], NEG)
    o_a = _paged_attn(page_table, ck_a, cv_a, f["q_a"].reshape(bsz, n_tok * A_HEADS, HEAD_DIM),
                      jnp.repeat(-fp_t, A_KV_HEADS, axis=-1), fn.reshape(bsz, n_tok * A_HEADS, 1),
                      rows128(f["k_a"], A_KV_HEADS), rows128(f["v_a"], A_KV_HEADS),
                      _new_key_bias(base, A_HEADS, A_KV_HEADS), rep_attn, A_KV_HEADS, A_HEADS, True)
    o_a = o_a.reshape(n, A_HEADS * HEAD_DIM).astype(BF16)

    qi = f["q_i"].reshape(IDX_HEADS, bsz, n_tok, IDX_DIM).transpose(1, 2, 0, 3).reshape(bsz, n_tok * IDX_HEADS, IDX_DIM)
    wi = f["w_i"].reshape(bsz, n_tok * IDX_HEADS, 1)
    kb, nbm = _sample_index(page_table, cki, qi, wi, _pad_rows(f["k_i"].reshape(bsz, n_tok, IDX_DIM), PAGE_SIZE),
                            rep_small, n_tok)
    key_bias = jnp.repeat(kb.transpose(0, 2, 1, 3).reshape(bsz, 8, past), B_KV_HEADS, axis=-1)
    base = jnp.broadcast_to(nbm[:, :n_tok, None, :n_tok], (bsz, n_tok, B_HEADS, n_tok))
    o_b = _paged_attn(page_table, ck_b, cv_b, f["q_b"].reshape(bsz, n_tok * B_HEADS, HEAD_DIM),
                      key_bias, jnp.zeros((bsz, n_tok * B_HEADS, 1), F32),
                      rows128(f["k_b"], B_KV_HEADS), rows128(f["v_b"], B_KV_HEADS),
                      _new_key_bias(base, B_HEADS, B_KV_HEADS), rep_attn, B_KV_HEADS, B_HEADS, False)
    o_b = o_b.reshape(n, B_HEADS * HEAD_DIM).astype(BF16)

    y = _back(x2, f, o_a, o_b, mods, p, tm, tm, _tile(n, 512), 512)
    return y.reshape(bsz, n_tok, d), f


def _layer_outputs(f, bsz, t):
    return (f["k_a"].reshape(bsz, t, A_KV_HEADS, HEAD_DIM), f["v_a"].reshape(bsz, t, A_KV_HEADS, HEAD_DIM),
            f["logf"].reshape(bsz, t, A_HEADS),
            f["k_b"].reshape(bsz, t, B_KV_HEADS, HEAD_DIM), f["v_b"].reshape(bsz, t, B_KV_HEADS, HEAD_DIM),
            f["k_i"].reshape(bsz, t, IDX_DIM))


def kernel(x_prompt, x_sample, cache_fox_k, cache_fox_v, cache_fox_logf, cache_dsa_k, cache_dsa_v, cache_idx_k, page_table, c_prompt, c_sample, w_ada, b_ada, g_norm1, g_norm2, w_in, b_fgate, g_qn_a, g_kn_a, g_qn_b, g_kn_b, w_branch_a, w_branch_b, w_out, w_peer_q, peer_subkeys, peer_u, peer_v):
    depth = w_ada.shape[0]
    bsz, t, d = x_prompt.shape
    bs, ts, _ = x_sample.shape
    xp, xs = x_prompt, x_sample
    rows_p, rows_s = [], []
    n_c = bsz + bs
    c_all = jnp.pad(jnp.concatenate([c_prompt, c_sample], axis=0), ((0, (-n_c) % 8), (0, 0)))
    n_pool = cache_fox_k.shape[1]
    pools = (cache_fox_k.reshape(-1, HEAD_DIM), cache_fox_v.reshape(-1, HEAD_DIM),
             cache_fox_logf.transpose(0, 1, 3, 2).reshape(depth * n_pool, A_HEADS, PAGE_SIZE),
             cache_dsa_k.reshape(-1, HEAD_DIM), cache_dsa_v.reshape(-1, HEAD_DIM),
             cache_idx_k.reshape(depth * n_pool, PAGE_SIZE, IDX_DIM))
    for l in range(depth):
        ada = _ada(c_all, w_ada[l], b_ada[l]).reshape(c_all.shape[0], N_ADA, d)
        w = _split_w_in(w_in[l], d)
        p = dict(g_norm1=g_norm1[l], g_norm2=g_norm2[l], b_fgate=b_fgate[l], g_qn_a=g_qn_a[l], g_kn_a=g_kn_a[l],
                 g_qn_b=g_qn_b[l], g_kn_b=g_kn_b[l],
                 w_branch_a=w_branch_a[l].astype(BF16), w_branch_b=w_branch_b[l].astype(BF16),
                 w_out=w_out[l].astype(BF16), wq_t=w_peer_q[l].T.astype(BF16),
                 subkeys=peer_subkeys[l].reshape(PEER_HEADS * 2, PEER_KEYS, -1).astype(BF16),
                 u16=peer_u[l].astype(BF16), vt16=peer_v[l].T.astype(BF16))
        xp, f_p = _prompt_group(xp, ada[:bsz], w, p)
        xs, f_s = _sample_group(xs, ada[bsz:n_c], w, p, pools, page_table + l * n_pool)
        rows_p.append(_layer_outputs(f_p, bsz, t))
        rows_s.append(_layer_outputs(f_s, bs, ts))
    outs_p = [jnp.stack([r[i] for r in rows_p], axis=0) for i in range(6)]
    outs_s = [jnp.stack([r[i] for r in rows_s], axis=0) for i in range(6)]
    return (xp, xs, *outs_p, *outs_s)
```

```python
import functools

import jax
import jax.numpy as jnp
from jax import lax
from jax.experimental import pallas as pl
from jax.experimental.pallas import tpu as pltpu

F32 = jnp.float32
BF16 = jnp.bfloat16
I32 = jnp.int32

HEAD_DIM = 128
A_HEADS = 8
A_KV_HEADS = 4
B_HEADS = 8
B_KV_HEADS = 4
IDX_HEADS = 16
IDX_DIM = 64
ROPE_THETA = 500000.0
ROPE_FRAC = 4
TOPK_MAX = 256
PEER_KEYS = 128
PEER_HEADS = 8
PEER_TOPK = 16
PAGE_SIZE = 128
EPS = 1e-6
ATTN_SCALE = HEAD_DIM ** -0.5
N_ADA = 6
LANES = 128
MXU_COLS = 256
NEG = -1e30
LOWEST = -1.0e38
BISECT_ITERS = 40
VMEM_LIMIT = 56 * 1024 * 1024


def _cparams(*sem):
    return pltpu.CompilerParams(dimension_semantics=sem, vmem_limit_bytes=VMEM_LIMIT)


def _tile(n, pref, mult=LANES):
    best = None
    t = mult
    while t <= min(n, pref):
        if n % t == 0:
            best = t
        t += mult
    return best if best is not None else n


def _dot(a, b):
    return jnp.dot(a, b, preferred_element_type=F32)


def _dot_nt(a, b):
    return lax.dot_general(a, b, (((1,), (1,)), ((), ())), preferred_element_type=F32)


def _sigmoid(x):
    return 1.0 / (1.0 + jnp.exp(-x))


def _rep(n, pref):
    return max(r for r in range(1, min(n, pref) + 1) if n % r == 0)


def _ada_kernel(c_ref, w_ref, b_ref, o_ref):
    c = c_ref[...]
    s = c * _sigmoid(c)
    o_ref[...] = _dot(s.astype(BF16), w_ref[...].astype(BF16)) + b_ref[...]


def _ada(c_all, w_ada, b_ada):
    m, d = c_all.shape
    n = w_ada.shape[1]
    tn = _tile(n, 1024)
    return pl.pallas_call(
        _ada_kernel,
        grid=(n // tn,),
        in_specs=[pl.BlockSpec((m, d), lambda j: (0, 0)),
                  pl.BlockSpec((d, tn), lambda j: (0, j)),
                  pl.BlockSpec((1, tn), lambda j: (0, j))],
        out_specs=pl.BlockSpec((m, tn), lambda j: (0, j)),
        out_shape=jax.ShapeDtypeStruct((m, n), F32),
        compiler_params=_cparams("arbitrary"),
        name="ada",
    )(c_all, w_ada, b_ada.reshape(1, n))


def _rms_mod(x, g, sc, sh):
    y = x * lax.rsqrt(jnp.mean(x * x, axis=-1, keepdims=True) + EPS)
    return (y * g) * (1.0 + sc) + sh


def _prenorm_kernel(x_ref, g_ref, sc_ref, sh_ref, o_ref):
    o_ref[...] = _rms_mod(x_ref[...], g_ref[...], sc_ref[...], sh_ref[...]).astype(o_ref.dtype)


def _mod_spec(mod, tm, rows_per_group):
    r, d = mod.shape[1], mod.shape[2]
    return pl.BlockSpec((None, r, d), lambda i: ((i * tm) // rows_per_group, 0, 0))


def _prenorm(x, g, sc, sh, tm, rows_per_group):
    n, d = x.shape
    return pl.pallas_call(
        _prenorm_kernel,
        grid=(n // tm,),
        in_specs=[pl.BlockSpec((tm, d), lambda i: (i, 0)),
                  pl.BlockSpec((1, d), lambda i: (0, 0)),
                  _mod_spec(sc, tm, rows_per_group),
                  _mod_spec(sh, tm, rows_per_group)],
        out_specs=pl.BlockSpec((tm, d), lambda i: (i, 0)),
        out_shape=jax.ShapeDtypeStruct((n, d), BF16),
        compiler_params=_cparams("arbitrary"),
        name="prenorm",
    )(x, g.reshape(1, d), sc, sh)


def _rope_tables(pos, head_dim):
    rd = head_dim // ROPE_FRAC
    half = rd // 2
    inv = ROPE_THETA ** (-jnp.arange(half, dtype=F32) / half)
    ang = pos.astype(F32)[:, None] * inv[None, :]
    cos, sin = jnp.cos(ang), jnp.sin(ang)
    t = pos.shape[0]
    rest = head_dim - rd
    c = jnp.concatenate([cos, cos, jnp.ones((t, rest), F32)], axis=-1)
    sa = jnp.concatenate([-sin, jnp.zeros((t, half + rest), F32)], axis=-1)
    sb = jnp.concatenate([jnp.zeros((t, half), F32), sin, jnp.zeros((t, rest), F32)], axis=-1)
    reps = LANES // head_dim
    return tuple(jnp.tile(a, (1, reps)) for a in (c, sa, sb)), half


def _rope(y, c, sa, sb, half):
    return y * c + pltpu.roll(y, LANES - half, 1) * sa + pltpu.roll(y, half, 1) * sb


def _head_norm(chunk, gain):
    y = chunk * lax.rsqrt(jnp.mean(chunk * chunk, axis=-1, keepdims=True) + EPS)
    return y * gain


def _qkv_kernel(*refs, n_q, n_k, rope_half, with_bf16):
    h_ref, w_ref, gain_ref = refs[:3]
    pos = 3
    if rope_half:
        c_ref, sa_ref, sb_ref = refs[3:6]
        pos = 6
    q_ref, k_ref, v_ref = refs[pos:pos + 3]
    z = _dot(h_ref[...], w_ref[...])
    for ch in range(n_q + n_k):
        sl = slice(ch * LANES, (ch + 1) * LANES)
        y = _head_norm(z[:, sl], gain_ref[:, sl])
        if rope_half:
            y = _rope(y, c_ref[...], sa_ref[...], sb_ref[...], rope_half)
        if ch < n_q:
            q_ref[:, sl] = (y * ATTN_SCALE).astype(q_ref.dtype)
        else:
            k_ref[:, (ch - n_q) * LANES:(ch - n_q + 1) * LANES] = y
    v = z[:, (n_q + n_k) * LANES:]
    v_ref[...] = v
    if with_bf16:
        kb_ref, vb_ref = refs[pos + 3:pos + 5]
        kb_ref[...] = k_ref[...].astype(BF16)
        vb_ref[...] = v.astype(BF16)


def _qkv_proj(h, w, gain, tables, rope_half, tm, n_q, n_k, with_bf16):
    n, d = h.shape
    wq, wk = n_q * LANES, n_k * LANES
    width = w.shape[1]
    row = lambda i: (i, 0)
    const = lambda i: (0, 0)
    in_specs = [pl.BlockSpec((tm, d), row), pl.BlockSpec((d, width), const),
                pl.BlockSpec((1, wq + wk), const)]
    args = [h, w, gain]
    if rope_half:
        nt = tables[0].shape[0] // tm
        for t in tables:
            in_specs.append(pl.BlockSpec((tm, LANES), lambda i: (i % nt, 0)))
            args.append(t)
    out_shape = [jax.ShapeDtypeStruct((n, wq), BF16), jax.ShapeDtypeStruct((n, wk), F32),
                 jax.ShapeDtypeStruct((n, wk), F32)]
    out_specs = [pl.BlockSpec((tm, wq), row), pl.BlockSpec((tm, wk), row), pl.BlockSpec((tm, wk), row)]
    if with_bf16:
        out_shape += [jax.ShapeDtypeStruct((n, wk), BF16)] * 2
        out_specs += [pl.BlockSpec((tm, wk), row)] * 2
    return pl.pallas_call(
        functools.partial(_qkv_kernel, n_q=n_q, n_k=n_k, rope_half=rope_half, with_bf16=with_bf16),
        grid=(n // tm,), in_specs=in_specs, out_specs=out_specs, out_shape=out_shape,
        compiler_params=_cparams("arbitrary"), name="qkv_proj",
    )(*args)


def _qidx_kernel(h_ref, w_ref, c_ref, sa_ref, sb_ref, o_ref, *, half):
    z = _dot(h_ref[...], w_ref[...])
    for ch in range(IDX_HEADS // 2):
        y = _rope(z[:, ch * LANES:(ch + 1) * LANES], c_ref[...], sa_ref[...], sb_ref[...], half)
        o_ref[2 * ch] = y[:, :IDX_DIM].astype(o_ref.dtype)
        o_ref[2 * ch + 1] = y[:, IDX_DIM:].astype(o_ref.dtype)


def _qidx_proj(h, w, tables, half, tm):
    n, d = h.shape
    nt = tables[0].shape[0] // tm
    tab = pl.BlockSpec((tm, LANES), lambda i: (i % nt, 0))
    return pl.pallas_call(
        functools.partial(_qidx_kernel, half=half),
        grid=(n // tm,),
        in_specs=[pl.BlockSpec((tm, d), lambda i: (i, 0)),
                  pl.BlockSpec((d, IDX_HEADS * IDX_DIM), lambda i: (0, 0)), tab, tab, tab],
        out_specs=pl.BlockSpec((IDX_HEADS, tm, IDX_DIM), lambda i: (0, i, 0)),
        out_shape=jax.ShapeDtypeStruct((IDX_HEADS, n, IDX_DIM), BF16),
        compiler_params=_cparams("arbitrary"), name="qidx_proj",
    )(h, w, *tables)


def _misc_kernel(h_ref, w_ref, bf_ref, c_ref, sa_ref, sb_ref, logf_ref, ki_ref, wi_ref, *, half):
    z = _dot(h_ref[...], w_ref[...])
    f = z[:, :A_HEADS] + bf_ref[...]
    logf_ref[...] = jnp.minimum(f, 0.0) - jnp.log1p(jnp.exp(-jnp.abs(f)))
    y = _rope(z[:, LANES:2 * LANES], c_ref[...], sa_ref[...], sb_ref[...], half)
    ki_ref[...] = y[:, :IDX_DIM]
    wi_ref[...] = z[:, 2 * LANES:2 * LANES + IDX_HEADS]


def _misc_proj(h, w, b_fgate, tables, half, tm):
    n, d = h.shape
    nt = tables[0].shape[0] // tm
    tab = pl.BlockSpec((tm, LANES), lambda i: (i % nt, 0))
    row = lambda i: (i, 0)
    return pl.pallas_call(
        functools.partial(_misc_kernel, half=half),
        grid=(n // tm,),
        in_specs=[pl.BlockSpec((tm, d), row), pl.BlockSpec((d, 3 * LANES), lambda i: (0, 0)),
                  pl.BlockSpec((1, A_HEADS), lambda i: (0, 0)), tab, tab, tab],
        out_specs=[pl.BlockSpec((tm, A_HEADS), row), pl.BlockSpec((tm, IDX_DIM), row),
                   pl.BlockSpec((tm, IDX_HEADS), row)],
        out_shape=[jax.ShapeDtypeStruct((n, A_HEADS), F32), jax.ShapeDtypeStruct((n, IDX_DIM), F32),
                   jax.ShapeDtypeStruct((n, IDX_HEADS), F32)],
        compiler_params=_cparams("arbitrary"), name="misc_proj",
    )(h, w, b_fgate.reshape(1, A_HEADS), *tables)


def _gate_kernel(h_ref, w_ref, o_ref):
    o_ref[...] = _sigmoid(_dot(h_ref[...], w_ref[...]))


def _gate_proj(h, w, tm):
    n, d = h.shape
    width = w.shape[1]
    return pl.pallas_call(
        _gate_kernel, grid=(n // tm,),
        in_specs=[pl.BlockSpec((tm, d), lambda i: (i, 0)), pl.BlockSpec((d, width), lambda i: (0, 0))],
        out_specs=pl.BlockSpec((tm, width), lambda i: (i, 0)),
        out_shape=jax.ShapeDtypeStruct((n, width), F32),
        compiler_params=_cparams("arbitrary"), name="gate_proj",
    )(h, w)


def _cumsum_kernel(x_ref, o_ref, *, tc):
    t = x_ref.shape[-1]
    r = lax.broadcasted_iota(I32, (tc, tc), 0)
    c = lax.broadcasted_iota(I32, (tc, tc), 1)
    tri = jnp.where(r <= c, 1.0, 0.0).astype(F32)
    carry = jnp.zeros((x_ref.shape[0], 1), F32)
    for blk in range(t // tc):
        sl = slice(blk * tc, (blk + 1) * tc)
        cs = jnp.dot(x_ref[:, sl], tri, precision=lax.Precision.HIGHEST, preferred_element_type=F32) + carry
        o_ref[:, sl] = cs
        carry = cs[:, tc - 1:tc]


def _cumsum_lanes(x):
    b, h, t = x.shape
    tc = _tile(t, 512)
    return pl.pallas_call(
        functools.partial(_cumsum_kernel, tc=tc), grid=(b,),
        in_specs=[pl.BlockSpec((None, h, t), lambda i: (i, 0, 0))],
        out_specs=pl.BlockSpec((None, h, t), lambda i: (i, 0, 0)),
        out_shape=jax.ShapeDtypeStruct((b, h, t), F32),
        compiler_params=_cparams("arbitrary"), name="cumsum",
    )(x)


def _online_update(s, v16, m_ref, l_ref, acc_ref):
    m_prev = m_ref[...]
    m_new = jnp.maximum(m_prev, jnp.max(s, axis=-1, keepdims=True))
    alpha = jnp.exp(m_prev - m_new)
    p = jnp.exp(s - m_new)
    l_ref[...] = alpha * l_ref[...] + jnp.sum(p, axis=-1, keepdims=True)
    acc_ref[...] = alpha * acc_ref[...] + _dot(p.astype(BF16), v16)
    m_ref[...] = m_new


def _fox_kernel(q_ref, k_ref, v_ref, fq_ref, fk_ref, o_ref, m_sc, l_sc, acc_sc, *, tq, tk, group):
    i = pl.program_id(1)
    kk = pl.program_id(2)
    n_heads = q_ref.shape[1] // LANES

    @pl.when(kk == 0)
    def _():
        m_sc[...] = jnp.full(m_sc.shape, NEG, F32)
        l_sc[...] = jnp.zeros(l_sc.shape, F32)
        acc_sc[...] = jnp.zeros(acc_sc.shape, F32)

    @pl.when(kk * tk <= i * tq + tq - 1)
    def _():
        row = i * tq + lax.broadcasted_iota(I32, (tq, tk), 0)
        col = kk * tk + lax.broadcasted_iota(I32, (tq, tk), 1)
        causal = col <= row
        for h in range(n_heads):
            kv = slice((h // group) * LANES, (h // group + 1) * LANES)
            s = _dot_nt(q_ref[:, h * LANES:(h + 1) * LANES], k_ref[:, kv].astype(BF16))
            s = s + fq_ref[:, h:h + 1] - fk_ref[h:h + 1, :]
            s = jnp.where(causal, s, NEG)
            _online_update(s, v_ref[:, kv].astype(BF16), m_sc.at[h], l_sc.at[h], acc_sc.at[h])

    @pl.when(kk == pl.num_programs(2) - 1)
    def _():
        for h in range(n_heads):
            o_ref[:, h * LANES:(h + 1) * LANES] = (acc_sc[h] / l_sc[h]).astype(o_ref.dtype)


def _fox_prompt(q, k, v, fq, fk, bsz, t):
    tq = _tile(t, 256)
    tk = _tile(t, 1024)
    nq, nk = t // tq, t // tk
    kvw = A_KV_HEADS * LANES
    last_k = lambda i, kk: jnp.minimum(kk, (i * tq + tq - 1) // tk)
    return pl.pallas_call(
        functools.partial(_fox_kernel, tq=tq, tk=tk, group=A_HEADS // A_KV_HEADS),
        grid=(bsz, nq, nk),
        in_specs=[pl.BlockSpec((tq, A_HEADS * LANES), lambda b, i, kk: (b * nq + i, 0)),
                  pl.BlockSpec((tk, kvw), lambda b, i, kk: (b * nk + last_k(i, kk), 0)),
                  pl.BlockSpec((tk, kvw), lambda b, i, kk: (b * nk + last_k(i, kk), 0)),
                  pl.BlockSpec((None, tq, A_HEADS), lambda b, i, kk: (b, i, 0)),
                  pl.BlockSpec((None, A_HEADS, tk), lambda b, i, kk: (b, 0, last_k(i, kk)))],
        out_specs=pl.BlockSpec((tq, A_HEADS * LANES), lambda b, i, kk: (b * nq + i, 0)),
        out_shape=jax.ShapeDtypeStruct((bsz * t, A_HEADS * LANES), BF16),
        scratch_shapes=[pltpu.VMEM((A_HEADS, tq, 1), F32), pltpu.VMEM((A_HEADS, tq, 1), F32),
                        pltpu.VMEM((A_HEADS, tq, LANES), F32)],
        compiler_params=_cparams("arbitrary", "arbitrary", "arbitrary"),
        name="fox_prompt",
    )(q, k, v, fq, fk)


def _kth_value(count_ge, lo, hi, c_lo, k):
    def cond(st):
        return jnp.logical_and(st[0] < BISECT_ITERS, jnp.max(st[3]) > k)

    def body(st):
        it, lo, hi, c_lo = st
        mid = 0.5 * lo + 0.5 * hi
        c = count_ge(mid)
        ge = c >= k
        return it + 1, jnp.where(ge, mid, lo), jnp.where(ge, hi, mid), jnp.where(ge, c, c_lo)

    return lax.while_loop(cond, body, (jnp.int32(0), lo, hi, c_lo))[1]


def _dsa_kernel(qb_ref, qi_ref, wi_ref, ki_ref, kb_ref, vb_ref, o_ref, key_sc, m_sc, l_sc, acc_sc,
                *, tq, tk, topk):
    i = pl.program_id(1)
    nch = (i * tq + tq + tk - 1) // tk
    row = i * tq + lax.broadcasted_iota(I32, (tq, 1), 0)

    def scores(c, carry):
        mx, mn = carry
        kc = ki_ref[pl.ds(pl.multiple_of(c * tk, tk), tk), :].astype(BF16)
        acc = jnp.zeros((tq, tk), F32)
        for h in range(IDX_HEADS):
            acc = acc + wi_ref[:, h:h + 1] * jnp.maximum(_dot_nt(qi_ref[h], kc), 0.0)
        valid = (c * tk + lax.broadcasted_iota(I32, (1, tk), 1)) <= row
        key_sc[c] = jnp.where(valid, acc, -jnp.inf)
        mx = jnp.maximum(mx, jnp.max(jnp.where(valid, acc, -jnp.inf), axis=-1, keepdims=True))
        mn = jnp.minimum(mn, jnp.min(jnp.where(valid, acc, jnp.inf), axis=-1, keepdims=True))
        return mx, mn

    mx, mn = lax.fori_loop(0, nch, scores, (jnp.full((tq, 1), -jnp.inf, F32), jnp.full((tq, 1), jnp.inf, F32)))

    def count_ge(cand):
        def cbody(c, cnt):
            ge = jnp.where(key_sc[c] >= cand, 1.0, 0.0)
            for s in range(tk // LANES):
                cnt = cnt + ge[:, s * LANES:(s + 1) * LANES]
            return cnt
        cnt = lax.fori_loop(0, nch, cbody, jnp.zeros((tq, LANES), F32))
        return jnp.sum(cnt, axis=-1, keepdims=True)

    many = row >= topk
    lo0 = jnp.where(many, mn, LOWEST)
    hi0 = jnp.where(many, mx, LOWEST)
    c0 = jnp.where(many, (row + 1).astype(F32), 0.0)
    thr = _kth_value(count_ge, lo0, hi0, c0, float(topk))

    m_sc[...] = jnp.full(m_sc.shape, NEG, F32)
    l_sc[...] = jnp.zeros(l_sc.shape, F32)
    acc_sc[...] = jnp.zeros(acc_sc.shape, F32)

    def attend(c, carry):
        bias = jnp.where(key_sc[c] >= thr, 0.0, NEG)
        rows = pl.ds(pl.multiple_of(c * tk, tk), tk)
        for h in range(B_HEADS):
            j = h // (B_HEADS // B_KV_HEADS)
            k16 = kb_ref[rows, j * LANES:(j + 1) * LANES]
            v16 = vb_ref[rows, j * LANES:(j + 1) * LANES]
            s = _dot_nt(qb_ref[:, h * LANES:(h + 1) * LANES], k16) + bias
            _online_update(s, v16, m_sc.at[h], l_sc.at[h], acc_sc.at[h])
        return carry

    lax.fori_loop(0, nch, attend, 0)
    for h in range(B_HEADS):
        o_ref[:, h * LANES:(h + 1) * LANES] = (acc_sc[h] / l_sc[h]).astype(o_ref.dtype)


def _dsa_prompt(qb, qi, wi, ki, kb16, vb16, bsz, t):
    tq = _tile(t, 256)
    tk = _tile(t, 1024)
    nq = t // tq
    topk = min(TOPK_MAX, t // 4)
    kvw = B_KV_HEADS * LANES
    return pl.pallas_call(
        functools.partial(_dsa_kernel, tq=tq, tk=tk, topk=topk),
        grid=(bsz, nq),
        in_specs=[pl.BlockSpec((tq, B_HEADS * LANES), lambda b, i: (b * nq + i, 0)),
                  pl.BlockSpec((IDX_HEADS, tq, IDX_DIM), lambda b, i: (0, b * nq + i, 0)),
                  pl.BlockSpec((tq, IDX_HEADS), lambda b, i: (b * nq + i, 0)),
                  pl.BlockSpec((t, IDX_DIM), lambda b, i: (b, 0)),
                  pl.BlockSpec((t, kvw), lambda b, i: (b, 0)),
                  pl.BlockSpec((t, kvw), lambda b, i: (b, 0))],
        out_specs=pl.BlockSpec((tq, B_HEADS * LANES), lambda b, i: (b * nq + i, 0)),
        out_shape=jax.ShapeDtypeStruct((bsz * t, B_HEADS * LANES), BF16),
        scratch_shapes=[pltpu.VMEM((t // tk, tq, tk), F32),
                        pltpu.VMEM((B_HEADS, tq, 1), F32), pltpu.VMEM((B_HEADS, tq, 1), F32),
                        pltpu.VMEM((B_HEADS, tq, LANES), F32)],
        compiler_params=_cparams("arbitrary", "arbitrary"),
        name="dsa_prompt",
    )(qb, qi, wi, ki, kb16, vb16)


def _page_specs(block, n_rep):
    def spec(r):
        return pl.BlockSpec(block, lambda b, pg, pt: (pt[b, pg * n_rep + r],) + (0,) * (len(block) - 1))
    return [spec(r) for r in range(n_rep)]


def _spread_matrix(n_kv, lower_tri):
    r = lax.broadcasted_iota(I32, (PAGE_SIZE, PAGE_SIZE * n_kv), 0)
    key = lax.shift_right_logical(lax.broadcasted_iota(I32, (PAGE_SIZE, PAGE_SIZE * n_kv), 1),
                                  n_kv.bit_length() - 1)
    return jnp.where((r <= key) if lower_tri else (r == key), 1.0, 0.0)


def _fscan_kernel(pt_ref, *refs, n_rep, n_kv):
    pages = refs[:n_rep]
    new_ref, fp_ref, fn_ref, carry_sc = refs[n_rep:]
    pg = pl.program_id(1)
    h = new_ref.shape[0]
    wide = PAGE_SIZE * n_kv
    m = n_rep * h
    hp = lax.Precision.HIGHEST

    @pl.when(pg == 0)
    def _():
        carry_sc[...] = jnp.zeros(carry_sc.shape, F32)

    x = jnp.concatenate([pages[rr][...] for rr in range(n_rep)], axis=0)
    cs = jnp.dot(x, _spread_matrix(n_kv, True), precision=hp, preferred_element_type=F32)
    tot = cs[:, wide - 1:wide]
    i = lax.broadcasted_iota(I32, (m, m), 0)
    j = lax.broadcasted_iota(I32, (m, m), 1)
    sh = h.bit_length() - 1
    earlier_page = jnp.where(lax.shift_right_logical(j, sh) < lax.shift_right_logical(i, sh), 1.0, 0.0)
    lower = jnp.where((i & (h - 1)) == (j & (h - 1)), earlier_page, 0.0)
    pref = jnp.dot(lower, jnp.broadcast_to(tot, (m, LANES)), precision=hp, preferred_element_type=F32)[:, :1]
    carry = carry_sc[...]
    out = cs + (pref + jnp.concatenate([carry] * n_rep, axis=0))
    for rr in range(n_rep):
        fp_ref[:, rr * wide:(rr + 1) * wide] = -out[rr * h:(rr + 1) * h, :]
    carry = carry + pref[m - h:, :] + tot[m - h:, :]
    carry_sc[...] = carry

    @pl.when(pg == pl.num_programs(1) - 1)
    def _():
        r = lax.broadcasted_iota(I32, (PAGE_SIZE, PAGE_SIZE), 0)
        c = lax.broadcasted_iota(I32, (PAGE_SIZE, PAGE_SIZE), 1)
        tri = jnp.where(r <= c, 1.0, 0.0)
        fn_ref[...] = jnp.dot(new_ref[...], tri, precision=hp, preferred_element_type=F32) + carry


def _fscan(page_table, logf_pool_t, logf_new_t, n_rep, n_kv):
    bsz, n_pages = page_table.shape
    h = logf_pool_t.shape[1]
    wide = PAGE_SIZE * n_kv
    grid_spec = pltpu.PrefetchScalarGridSpec(
        num_scalar_prefetch=1, grid=(bsz, n_pages // n_rep),
        in_specs=_page_specs((None, h, PAGE_SIZE), n_rep)
        + [pl.BlockSpec((None, h, PAGE_SIZE), lambda b, pg, pt: (b, 0, 0))],
        out_specs=[pl.BlockSpec((None, h, n_rep * wide), lambda b, pg, pt: (b, 0, pg)),
                   pl.BlockSpec((None, h, PAGE_SIZE), lambda b, pg, pt: (b, 0, 0))],
        scratch_shapes=[pltpu.VMEM((h, 1), F32)])
    return pl.pallas_call(
        functools.partial(_fscan_kernel, n_rep=n_rep, n_kv=n_kv), grid_spec=grid_spec,
        out_shape=[jax.ShapeDtypeStruct((bsz, h, n_pages * wide), F32),
                   jax.ShapeDtypeStruct((bsz, h, PAGE_SIZE), F32)],
        compiler_params=_cparams("arbitrary", "arbitrary"), name="fscan",
    )(page_table, *([logf_pool_t] * n_rep), logf_new_t)


def _sidx_kernel(pt_ref, *refs, n_rep, n_tok, topk, n_kv):
    pages = refs[:n_rep]
    qi_ref, wi_ref, kn_ref, kb_ref, nb_ref, sc_sc = refs[n_rep:]
    pg = pl.program_id(1)
    npg = pl.num_programs(1)
    rows = 8
    pad = jnp.zeros((rows - n_tok, PAGE_SIZE), F32)

    def score(keys_f32):
        d = jnp.maximum(_dot_nt(qi_ref[...], keys_f32.astype(BF16)), 0.0) * wi_ref[...]
        return jnp.concatenate([jnp.sum(d.reshape(n_tok, IDX_HEADS, PAGE_SIZE), axis=1), pad], axis=0)

    sc_sc[pg] = jnp.concatenate([score(pages[rr][...]) for rr in range(n_rep)], axis=1)

    @pl.when(pg == npg - 1)
    def _():
        t_idx = lax.broadcasted_iota(I32, (rows, PAGE_SIZE), 0)
        s_idx = lax.broadcasted_iota(I32, (rows, PAGE_SIZE), 1)
        new_valid = (s_idx <= t_idx) & (t_idx < n_tok)
        new_raw = score(kn_ref[...])
        new = jnp.where(new_valid, new_raw, -jnp.inf)
        past = sc_sc[...]

        def count_ge(cand):
            c_past = jnp.sum(jnp.sum(jnp.where(past >= cand[None], 1.0, 0.0), axis=0), axis=-1, keepdims=True)
            return c_past + jnp.sum(jnp.where(new >= cand, 1.0, 0.0), axis=-1, keepdims=True)

        mx = jnp.maximum(jnp.max(jnp.max(past, axis=0), axis=-1, keepdims=True),
                         jnp.max(new, axis=-1, keepdims=True))
        mn = jnp.minimum(jnp.min(jnp.min(past, axis=0), axis=-1, keepdims=True),
                         jnp.min(jnp.where(new_valid, new_raw, jnp.inf), axis=-1, keepdims=True))
        tok = lax.broadcasted_iota(I32, (rows, 1), 0)
        n_valid = jnp.where(tok < n_tok, (past.shape[0] * past.shape[2] + 1 + tok).astype(F32), 0.0)
        many = n_valid > topk
        thr = _kth_value(count_ge, jnp.where(many, mn, LOWEST), jnp.where(many, mx, LOWEST),
                         jnp.where(many, n_valid, 0.0), float(topk))
        spread = _spread_matrix(n_kv, False).astype(BF16)
        wide = PAGE_SIZE * n_kv
        for s in range(past.shape[0]):
            sel = jnp.where(past[s] >= thr, 1.0, 0.0)
            stacked = jnp.concatenate([sel[:, c * PAGE_SIZE:(c + 1) * PAGE_SIZE] for c in range(n_rep)], axis=0)
            ex = _dot(stacked.astype(BF16), spread)
            for c in range(n_rep):
                kb_ref[:, (s * n_rep + c) * wide:(s * n_rep + c + 1) * wide] = jnp.where(
                    ex[c * rows:(c + 1) * rows, :] > 0.5, 0.0, NEG)
        nb_ref[...] = jnp.where(new >= thr, 0.0, NEG)


def _sample_index(page_table, idx_pool, qi, wi, ki_new, n_rep, n_tok, n_kv):
    bsz, n_pages = page_table.shape
    p = n_pages * PAGE_SIZE
    topk = min(TOPK_MAX, (p + n_tok) // 4)
    nr = n_tok * IDX_HEADS
    steps = n_pages // n_rep
    wide = n_rep * PAGE_SIZE
    grid_spec = pltpu.PrefetchScalarGridSpec(
        num_scalar_prefetch=1, grid=(bsz, steps),
        in_specs=_page_specs((None, PAGE_SIZE, IDX_DIM), n_rep)
        + [pl.BlockSpec((None, nr, IDX_DIM), lambda b, pg, pt: (b, 0, 0)),
           pl.BlockSpec((None, nr, 1), lambda b, pg, pt: (b, 0, 0)),
           pl.BlockSpec((None, PAGE_SIZE, IDX_DIM), lambda b, pg, pt: (b, 0, 0))],
        out_specs=[pl.BlockSpec((None, 8, p * n_kv), lambda b, pg, pt: (b, 0, 0)),
                   pl.BlockSpec((None, 8, PAGE_SIZE), lambda b, pg, pt: (b, 0, 0))],
        scratch_shapes=[pltpu.VMEM((steps, 8, wide), F32)])
    return pl.pallas_call(
        functools.partial(_sidx_kernel, n_rep=n_rep, n_tok=n_tok, topk=topk, n_kv=n_kv), grid_spec=grid_spec,
        out_shape=[jax.ShapeDtypeStruct((bsz, 8, p * n_kv), F32),
                   jax.ShapeDtypeStruct((bsz, 8, PAGE_SIZE), F32)],
        compiler_params=_cparams("arbitrary", "arbitrary"), name="sample_index",
    )(page_table, *([idx_pool] * n_rep), qi, wi, ki_new)


def _paged_attn_kernel(pt_ref, *refs, n_rep, n_kv, n_heads, per_head):
    kp = refs[:n_rep]
    vp = refs[n_rep:2 * n_rep]
    q_ref, kb_ref, rb_ref, kn_ref, vn_ref, nb_ref, o_ref, m_sc, l_sc, acc_sc = refs[2 * n_rep:]
    pg = pl.program_id(1)
    nrow = q_ref.shape[0]
    wide = PAGE_SIZE * n_kv

    @pl.when(pg == 0)
    def _():
        m_sc[...] = jnp.full(m_sc.shape, NEG, F32)
        l_sc[...] = jnp.zeros(l_sc.shape, F32)
        acc_sc[...] = jnp.zeros(acc_sc.shape, F32)

    def expand(kb):
        if per_head:
            return jnp.concatenate([kb] * (nrow // kb.shape[0]), axis=0)
        return jnp.concatenate([jnp.broadcast_to(kb[t:t + 1], (8, kb.shape[1])) for t in range(nrow // 8)], axis=0)

    head = lax.broadcasted_iota(I32, (nrow, wide), 0) % n_heads
    kv_col = lax.broadcasted_iota(I32, (nrow, wide), 1) % n_kv
    head_bias = jnp.where(head // (n_heads // n_kv) == kv_col, 0.0, NEG)
    q = q_ref[...]
    rb = rb_ref[...]

    def update(s, vs, width):
        m_prev = m_sc[...]
        m_new = jnp.maximum(m_prev, jnp.max(s, axis=-1, keepdims=True))
        alpha = jnp.exp(m_prev - m_new)
        p = jnp.exp(s - m_new)
        l_sc[...] = alpha * l_sc[...] + jnp.sum(p, axis=-1, keepdims=True)
        acc = alpha * acc_sc[...]
        for r, v in enumerate(vs):
            acc = acc + _dot(p[:, r * width:(r + 1) * width].astype(BF16), v.astype(BF16))
        acc_sc[...] = acc
        m_sc[...] = m_new

    s = jnp.concatenate([_dot_nt(q, kp[r][...].astype(BF16)) for r in range(n_rep)], axis=1)
    bias = expand(kb_ref[...]) + jnp.concatenate([head_bias] * n_rep, axis=1)
    update(s + rb + bias, [vp[r][...] for r in range(n_rep)], wide)

    @pl.when(pg == pl.num_programs(1) - 1)
    def _():
        sn = _dot_nt(q, kn_ref[...].astype(BF16)) + rb + nb_ref[...]
        update(sn, [vn_ref[...]], PAGE_SIZE)
        o_ref[...] = acc_sc[...] / l_sc[...]


def _paged_attn(page_table, k_pool, v_pool, q, key_bias, row_bias, k_new, v_new, new_bias, n_rep, n_kv, n_heads,
                per_head):
    bsz, n_pages = page_table.shape
    nrow, w = q.shape[1], q.shape[2]
    kbh = key_bias.shape[1]
    wide = PAGE_SIZE * n_kv
    const = lambda b, pg, pt: (b, 0, 0)
    grid_spec = pltpu.PrefetchScalarGridSpec(
        num_scalar_prefetch=1, grid=(bsz, n_pages // n_rep),
        in_specs=_page_specs((wide, w), n_rep) + _page_specs((wide, w), n_rep)
        + [pl.BlockSpec((None, nrow, w), const),
           pl.BlockSpec((None, kbh, n_rep * wide), lambda b, pg, pt: (b, 0, pg)),
           pl.BlockSpec((None, nrow, 1), const),
           pl.BlockSpec((None, PAGE_SIZE, w), const),
           pl.BlockSpec((None, PAGE_SIZE, w), const),
           pl.BlockSpec((None, nrow, PAGE_SIZE), const)],
        out_specs=pl.BlockSpec((None, nrow, w), const),
        scratch_shapes=[pltpu.VMEM((nrow, 1), F32), pltpu.VMEM((nrow, 1), F32), pltpu.VMEM((nrow, w), F32)])
    return pl.pallas_call(
        functools.partial(_paged_attn_kernel, n_rep=n_rep, n_kv=n_kv, n_heads=n_heads, per_head=per_head),
        grid_spec=grid_spec,
        out_shape=jax.ShapeDtypeStruct((bsz, nrow, w), F32),
        compiler_params=_cparams("arbitrary", "arbitrary"), name="paged_attn",
    )(page_table, *([k_pool] * n_rep), *([v_pool] * n_rep), q, key_bias, row_bias, k_new, v_new, new_bias)


def _merge_kernel(oa_ref, ob_ref, ga_ref, gb_ref, wa_ref, wb_ref, o_ref):
    m = ga_ref[...] * _dot(oa_ref[...], wa_ref[...]) + gb_ref[...] * _dot(ob_ref[...], wb_ref[...])
    o_ref[...] = m.astype(o_ref.dtype)


def _merge(oa, ob, ga, gb, wa, wb, tm):
    n, e = oa.shape
    d = wa.shape[1]
    row = lambda i: (i, 0)
    const = lambda i: (0, 0)
    return pl.pallas_call(
        _merge_kernel, grid=(n // tm,),
        in_specs=[pl.BlockSpec((tm, e), row), pl.BlockSpec((tm, e), row),
                  pl.BlockSpec((tm, d), row), pl.BlockSpec((tm, d), row),
                  pl.BlockSpec((e, d), const), pl.BlockSpec((e, d), const)],
        out_specs=pl.BlockSpec((tm, d), row),
        out_shape=jax.ShapeDtypeStruct((n, d), BF16),
        compiler_params=_cparams("arbitrary"), name="merge",
    )(oa, ob, ga, gb, wa, wb)


def _outproj_kernel(x_ref, m_ref, w_ref, g1_ref, gn_ref, sc_ref, sh_ref, x1_ref, h2t_ref):
    x1 = x_ref[...] + g1_ref[...] * _dot(m_ref[...], w_ref[...])
    x1_ref[...] = x1
    h2t_ref[...] = _rms_mod(x1, gn_ref[...], sc_ref[...], sh_ref[...]).T.astype(h2t_ref.dtype)


def _outproj(x, merged, w_out, ga1, g_norm2, sc2, sh2, tm, rows_per_group):
    n, d = x.shape
    row = lambda i: (i, 0)
    return pl.pallas_call(
        _outproj_kernel, grid=(n // tm,),
        in_specs=[pl.BlockSpec((tm, d), row), pl.BlockSpec((tm, d), row),
                  pl.BlockSpec((d, d), lambda i: (0, 0)),
                  _mod_spec(ga1, tm, rows_per_group),
                  pl.BlockSpec((1, d), lambda i: (0, 0)),
                  _mod_spec(sc2, tm, rows_per_group), _mod_spec(sh2, tm, rows_per_group)],
        out_specs=[pl.BlockSpec((tm, d), row), pl.BlockSpec((d, tm), lambda i: (0, i))],
        out_shape=[jax.ShapeDtypeStruct((n, d), F32), jax.ShapeDtypeStruct((d, n), BF16)],
        compiler_params=_cparams("arbitrary"), name="outproj",
    )(x, merged, w_out, ga1, g_norm2.reshape(1, d), sc2, sh2)


def _top_values(x, n):
    out = []
    for _ in range(n):
        m = jnp.max(x, axis=0, keepdims=True)
        out.append(m)
        x = jnp.where(x == m, -jnp.inf, x)
    return out


def _peer_query_kernel(ht_ref, wq_ref, sk_ref, s_ref, thr_ref, m1_ref, m2_ref, rz_ref, sv_sc):
    qt = _dot(wq_ref[...], ht_ref[...])
    n_hc = sk_ref.shape[0]
    for hc in range(n_hc):
        s = _dot(sk_ref[hc], qt[hc * PEER_KEYS:(hc + 1) * PEER_KEYS, :].astype(BF16))
        s_ref[hc] = s
        for r, m in enumerate(_top_values(s, PEER_TOPK)):
            sv_sc[hc, r:r + 1, :] = m
    for h in range(n_hc // 2):
        sv1 = sv_sc[2 * h]
        sv2 = sv_sc[2 * h + 1]
        cands = [sv1[0:1] + sv2]
        cands += [sv1[a:a + 1] + sv2[0:8] for a in range(1, 8)]
        cands += [sv1[8:16] + sv2[0:1]]
        tops = _top_values(jnp.concatenate(cands, axis=0), PEER_TOPK)
        z = jnp.zeros_like(tops[0])
        for m in tops:
            z = z + jnp.exp(m - tops[0])
        thr_ref[h:h + 1, :] = tops[-1]
        m1_ref[h:h + 1, :] = sv1[0:1]
        m2_ref[h:h + 1, :] = sv2[0:1]
        rz_ref[h:h + 1, :] = 1.0 / z


def _peer_query(h2t, wqt, subkeys, tm):
    d, n = h2t.shape
    n_hc = subkeys.shape[0]
    col = lambda i: (0, i)
    small = jax.ShapeDtypeStruct((PEER_HEADS, n), F32)
    small_spec = pl.BlockSpec((PEER_HEADS, tm), col)
    return pl.pallas_call(
        _peer_query_kernel, grid=(n // tm,),
        in_specs=[pl.BlockSpec((d, tm), col), pl.BlockSpec(wqt.shape, lambda i: (0, 0)),
                  pl.BlockSpec(subkeys.shape, lambda i: (0, 0, 0))],
        out_specs=[pl.BlockSpec((n_hc, PEER_KEYS, tm), lambda i: (0, 0, i))] + [small_spec] * 4,
        out_shape=[jax.ShapeDtypeStruct((n_hc, PEER_KEYS, n), F32)] + [small] * 4,
        scratch_shapes=[pltpu.VMEM((n_hc, PEER_TOPK, tm), F32)],
        compiler_params=_cparams("arbitrary"), name="peer_query",
    )(h2t, wqt, subkeys)


def _peer_weighted_acts(tile, live, a_ref, wa_ref, s_ref, thr_ref, m1_ref, e2_sc):
    ts, tm = a_ref.shape
    half = PEER_KEYS // 2

    def block(q, l):
        def run():
            i1 = jnp.clip(tile * (ts // PEER_KEYS) + q, 0, PEER_KEYS - 1)
            lanes = slice(l * LANES, (l + 1) * LANES)
            s1 = [s_ref[2 * h, pl.ds(i1, 1), :][:, lanes] for h in range(PEER_HEADS)]
            e1 = [jnp.exp(s1[h] - m1_ref[h:h + 1, lanes]) for h in range(PEER_HEADS)]
            thr = [jnp.where(live, thr_ref[h:h + 1, lanes], jnp.inf) for h in range(PEER_HEADS)]
            halves = []
            for r in range(2):
                rows = slice(r * half, (r + 1) * half)
                w = jnp.zeros((half, LANES), F32)
                for h in range(PEER_HEADS):
                    sel = (s_ref[2 * h + 1, rows, lanes] + s1[h]) >= thr[h]
                    w = w + jnp.where(sel, e2_sc[h, rows, lanes] * e1[h], 0.0)
                a = a_ref[q * PEER_KEYS + r * half:q * PEER_KEYS + (r + 1) * half, lanes]
                halves.append(w * (0.5 * a * (1.0 + lax.erf(a * (2.0 ** -0.5)))))
            blk = jnp.concatenate(halves, axis=0)
            wa_ref[lanes, q * PEER_KEYS:(q + 1) * PEER_KEYS] = blk.T.astype(wa_ref.dtype)
        return run

    return [block(q, l) for q in range(ts // PEER_KEYS) for l in range(tm // LANES)]


def _interleave(vpu_tasks, mxu_tasks):
    n_v, n_m = len(vpu_tasks), len(mxu_tasks)
    done_m = 0
    for i, task in enumerate(vpu_tasks):
        task()
        while done_m * n_v < (i + 1) * n_m:
            mxu_tasks[done_m]()
            done_m += 1


def _peer_dense_kernel(ht_ref, u_ref, v_ref, s_ref, thr_ref, m1_ref, m2_ref, rz_ref, x1_ref, g2_ref, o_ref,
                       a0, a1, wa0, wa1, e2_sc):
    k = pl.program_id(1)
    last = pl.num_programs(1) - 1
    ts = a0.shape[0]

    @pl.when(k == 0)
    def _():
        o_ref[...] = jnp.zeros(o_ref.shape, F32)
        wa0[...] = jnp.zeros(wa0.shape, wa0.dtype)
        a1[...] = jnp.zeros(a1.shape, F32)
        for h in range(PEER_HEADS):
            e2_sc[h] = jnp.exp(s_ref[2 * h + 1] - m2_ref[h:h + 1, :]) * rz_ref[h:h + 1, :]

    d = o_ref.shape[1]
    cw = min(d, MXU_COLS)
    rh = ts // 2

    def act_chunk(a_ref, base, c):
        def run():
            a_ref[c * rh:(c + 1) * rh, :] = _dot(u_ref[base + c * rh:base + (c + 1) * rh, :], ht_ref[...])
        return run

    def val_chunk(wa_ref, base, c):
        def run():
            o_ref[:, c * cw:(c + 1) * cw] += _dot(wa_ref[...], v_ref[base:base + ts, c * cw:(c + 1) * cw])
        return run

    weights = lambda tile, live, a_ref, wa_ref: _peer_weighted_acts(tile, live, a_ref, wa_ref, s_ref, thr_ref,
                                                                    m1_ref, e2_sc)
    _interleave(weights(2 * k - 1, k >= 1, a1, wa1),
                [act_chunk(a0, 0, c) for c in range(2)] + [val_chunk(wa0, 0, c) for c in range(d // cw)])
    _interleave(weights(2 * k, k < last, a0, wa0),
                [act_chunk(a1, ts, c) for c in range(2)] + [val_chunk(wa1, ts, c) for c in range(d // cw)])

    @pl.when(k == last)
    def _():
        o_ref[...] = x1_ref[...] + g2_ref[...] * o_ref[...]


def _peer_dense(h2t, u16, v16, s_t, thr, m1, m2, rz, x1, ga2, tm, ts, rows_per_group):
    d, n = h2t.shape
    n_exp = u16.shape[0]
    n_hc = s_t.shape[0]
    n_k = n_exp // (2 * ts)
    tok = lambda i, k: (0, i)
    small_spec = pl.BlockSpec((PEER_HEADS, tm), tok)
    g2_spec = pl.BlockSpec((None,) + ga2.shape[1:], lambda i, k: ((i * tm) // rows_per_group, 0, 0))
    return pl.pallas_call(
        _peer_dense_kernel,
        grid=(n // tm, n_k + 1),
        in_specs=[pl.BlockSpec((d, tm), tok),
                  pl.BlockSpec((2 * ts, d), lambda i, k: (jnp.minimum(k, n_k - 1), 0)),
                  pl.BlockSpec((2 * ts, d), lambda i, k: (jnp.maximum(k - 1, 0), 0)),
                  pl.BlockSpec((n_hc, PEER_KEYS, tm), lambda i, k: (0, 0, i))] + [small_spec] * 4
        + [pl.BlockSpec((tm, d), lambda i, k: (i, 0)), g2_spec],
        out_specs=pl.BlockSpec((tm, d), lambda i, k: (i, 0)),
        out_shape=jax.ShapeDtypeStruct((n, d), F32),
        scratch_shapes=[pltpu.VMEM((ts, tm), F32), pltpu.VMEM((ts, tm), F32),
                        pltpu.VMEM((tm, ts), BF16), pltpu.VMEM((tm, ts), BF16),
                        pltpu.VMEM((PEER_HEADS, PEER_KEYS, tm), F32)],
        compiler_params=_cparams("arbitrary", "arbitrary"), name="peer_dense",
    )(h2t, u16, v16, s_t, thr, m1, m2, rz, x1, ga2)


def _split_w_in(w_in, d_model):
    sizes = (A_HEADS * HEAD_DIM, A_KV_HEADS * HEAD_DIM, A_KV_HEADS * HEAD_DIM, A_HEADS,
             B_HEADS * HEAD_DIM, B_KV_HEADS * HEAD_DIM, B_KV_HEADS * HEAD_DIM,
             IDX_HEADS * IDX_DIM, IDX_DIM, IDX_HEADS, d_model, d_model)
    offs = [0]
    for s in sizes:
        offs.append(offs[-1] + s)
    cols = lambda a, b: w_in[:, offs[a]:offs[b]].astype(BF16)
    padw = lambda a: jnp.pad(w_in[:, offs[a]:offs[a + 1]], ((0, 0), (0, LANES - sizes[a]))).astype(BF16)
    return dict(a=cols(0, 3), b=cols(4, 7), qi=cols(7, 8),
                misc=jnp.concatenate([padw(3), padw(8), padw(9)], axis=1),
                gate_a=cols(10, 11), gate_b=cols(11, 12))


def _front(x2, mods, pos, w, p, tm, rows_per_group):
    sh1, sc1 = mods[0], mods[1]
    h = _prenorm(x2, p["g_norm1"], sc1, sh1, tm, rows_per_group)
    tab128, half128 = _rope_tables(pos, HEAD_DIM)
    tab64, half64 = _rope_tables(pos, IDX_DIM)
    gain_a = jnp.concatenate([jnp.tile(p["g_qn_a"], A_HEADS), jnp.tile(p["g_kn_a"], A_KV_HEADS)]).reshape(1, -1)
    gain_b = jnp.concatenate([jnp.tile(p["g_qn_b"], B_HEADS), jnp.tile(p["g_kn_b"], B_KV_HEADS)]).reshape(1, -1)
    q_a, k_a, v_a = _qkv_proj(h, w["a"], gain_a, None, 0, tm, A_HEADS, A_KV_HEADS, False)
    q_b, k_b, v_b, kb16, vb16 = _qkv_proj(h, w["b"], gain_b, tab128, half128, tm, B_HEADS, B_KV_HEADS, True)
    q_i = _qidx_proj(h, w["qi"], tab64, half64, tm)
    logf, k_i, w_i = _misc_proj(h, w["misc"], p["b_fgate"], tab64, half64, tm)
    gate_a = _gate_proj(h, w["gate_a"], tm)
    gate_b = _gate_proj(h, w["gate_b"], tm)
    return dict(q_a=q_a, k_a=k_a, v_a=v_a, logf=logf, q_b=q_b, k_b=k_b, v_b=v_b, kb16=kb16, vb16=vb16,
                q_i=q_i, k_i=k_i, w_i=w_i, gate_a=gate_a, gate_b=gate_b)


def _back(x2, f, o_a, o_b, mods, p, tm, rows_per_group, tm_peer, te):
    ga1, sh2, sc2, ga2 = mods[2], mods[3], mods[4], mods[5]
    merged = _merge(o_a, o_b, f["gate_a"], f["gate_b"], p["w_branch_a"], p["w_branch_b"], tm)
    x1, h2t = _outproj(x2, merged, p["w_out"], ga1, p["g_norm2"], sc2, sh2, tm, rows_per_group)
    s_t, thr, m1, m2, rz = _peer_query(h2t, p["wq_t"], p["subkeys"], tm_peer)
    return _peer_dense(h2t, p["u16"], p["v16"], s_t, thr, m1, m2, rz, x1, ga2, tm_peer, te, rows_per_group)


def _prompt_group(x, ada, w, p):
    bsz, t, d = x.shape
    tm = _tile(t, 512)
    mods = [ada[:, i].reshape(bsz, 1, d) for i in range(N_ADA)]
    x2 = x.reshape(bsz * t, d)
    f = _front(x2, mods, jnp.arange(t), w, p, tm, t)
    fk = _cumsum_lanes(f["logf"].reshape(bsz, t, A_HEADS).transpose(0, 2, 1))
    o_a = _fox_prompt(f["q_a"], f["k_a"], f["v_a"], fk.transpose(0, 2, 1), fk, bsz, t)
    o_b = _dsa_prompt(f["q_b"], f["q_i"], f["w_i"], f["k_i"], f["kb16"], f["vb16"], bsz, t)
    y = _back(x2, f, o_a, o_b, mods, p, tm, t, _tile(bsz * t, 512), 512)
    return y.reshape(bsz, t, d), f


def _new_key_bias(base, n_heads, n_kv):
    bsz, n_tok = base.shape[0], base.shape[1]
    head_ok = (jnp.arange(n_heads)[:, None] // (n_heads // n_kv)) == jnp.arange(n_kv)[None, :]
    full = jnp.where(head_ok[None, None, :, None, :], base[..., None], NEG)
    full = full.reshape(bsz, n_tok * n_heads, n_tok * n_kv)
    return jnp.pad(full, ((0, 0), (0, 0), (0, PAGE_SIZE - n_tok * n_kv)), constant_values=NEG)


def _pad_rows(a, rows):
    return jnp.pad(a, ((0, 0), (0, rows - a.shape[1]), (0, 0)))


def _sample_group(x, ada, w, p, caches, page_table):
    bsz, n_tok, d = x.shape
    n = bsz * n_tok
    ck_a, cv_a, clf, ck_b, cv_b, cki = caches
    n_pages = page_table.shape[1]
    past = n_pages * PAGE_SIZE
    tm = _tile(n, 512)
    mods = [jnp.repeat(ada[:, i], n_tok, axis=0).reshape(n // tm, tm, d) for i in range(N_ADA)]
    x2 = x.reshape(n, d)
    pos = jnp.tile(past + jnp.arange(n_tok), bsz)
    f = _front(x2, mods, pos, w, p, tm, tm)
    rep_attn, rep_small = _rep(n_pages, 16), _rep(n_pages, 32)
    t_idx = jnp.arange(n_tok)
    causal = t_idx[None, :] <= t_idx[:, None]

    def rows128(a, n_kv):
        return _pad_rows(a.reshape(bsz, n_tok * n_kv, HEAD_DIM), PAGE_SIZE)

    lf_new = jnp.pad(f["logf"].reshape(bsz, n_tok, A_HEADS).transpose(0, 2, 1),
                     ((0, 0), (0, 0), (0, PAGE_SIZE - n_tok)))
    neg_fp, fn_t = _fscan(page_table, clf, lf_new, rep_small, A_KV_HEADS)
    fn = fn_t[:, :, :n_tok].transpose(0, 2, 1)
    base = jnp.where(causal[None, :, None, :], -fn.transpose(0, 2, 1)[:, None, :, :], NEG)
    o_a = _paged_attn(page_table, ck_a, cv_a, f["q_a"].reshape(bsz, n_tok * A_HEADS, HEAD_DIM),
                      neg_fp, fn.reshape(bsz, n_tok * A_HEADS, 1),
                      rows128(f["k_a"], A_KV_HEADS), rows128(f["v_a"], A_KV_HEADS),
                      _new_key_bias(base, A_HEADS, A_KV_HEADS), rep_attn, A_KV_HEADS, A_HEADS, True)
    o_a = o_a.reshape(n, A_HEADS * HEAD_DIM).astype(BF16)

    qi = f["q_i"].reshape(IDX_HEADS, bsz, n_tok, IDX_DIM).transpose(1, 2, 0, 3).reshape(bsz, n_tok * IDX_HEADS, IDX_DIM)
    wi = f["w_i"].reshape(bsz, n_tok * IDX_HEADS, 1)
    key_bias, nbm = _sample_index(page_table, cki, qi, wi,
                                  _pad_rows(f["k_i"].reshape(bsz, n_tok, IDX_DIM), PAGE_SIZE),
                                  rep_small, n_tok, B_KV_HEADS)
    base = jnp.broadcast_to(nbm[:, :n_tok, None, :n_tok], (bsz, n_tok, B_HEADS, n_tok))
    o_b = _paged_attn(page_table, ck_b, cv_b, f["q_b"].reshape(bsz, n_tok * B_HEADS, HEAD_DIM),
                      key_bias, jnp.zeros((bsz, n_tok * B_HEADS, 1), F32),
                      rows128(f["k_b"], B_KV_HEADS), rows128(f["v_b"], B_KV_HEADS),
                      _new_key_bias(base, B_HEADS, B_KV_HEADS), rep_attn, B_KV_HEADS, B_HEADS, False)
    o_b = o_b.reshape(n, B_HEADS * HEAD_DIM).astype(BF16)

    y = _back(x2, f, o_a, o_b, mods, p, tm, tm, _tile(n, 512), 512)
    return y.reshape(bsz, n_tok, d), f


def _layer_outputs(f, bsz, t):
    return (f["k_a"].reshape(bsz, t, A_KV_HEADS, HEAD_DIM), f["v_a"].reshape(bsz, t, A_KV_HEADS, HEAD_DIM),
            f["logf"].reshape(bsz, t, A_HEADS),
            f["k_b"].reshape(bsz, t, B_KV_HEADS, HEAD_DIM), f["v_b"].reshape(bsz, t, B_KV_HEADS, HEAD_DIM),
            f["k_i"].reshape(bsz, t, IDX_DIM))


def kernel(x_prompt, x_sample, cache_fox_k, cache_fox_v, cache_fox_logf, cache_dsa_k, cache_dsa_v, cache_idx_k, page_table, c_prompt, c_sample, w_ada, b_ada, g_norm1, g_norm2, w_in, b_fgate, g_qn_a, g_kn_a, g_qn_b, g_kn_b, w_branch_a, w_branch_b, w_out, w_peer_q, peer_subkeys, peer_u, peer_v):
    depth = w_ada.shape[0]
    bsz, t, d = x_prompt.shape
    bs, ts, _ = x_sample.shape
    xp, xs = x_prompt, x_sample
    rows_p, rows_s = [], []
    n_c = bsz + bs
    c_all = jnp.pad(jnp.concatenate([c_prompt, c_sample], axis=0), ((0, (-n_c) % 8), (0, 0)))
    n_pool = cache_fox_k.shape[1]
    pools = (cache_fox_k.reshape(-1, HEAD_DIM), cache_fox_v.reshape(-1, HEAD_DIM),
             cache_fox_logf.transpose(0, 1, 3, 2).reshape(depth * n_pool, A_HEADS, PAGE_SIZE),
             cache_dsa_k.reshape(-1, HEAD_DIM), cache_dsa_v.reshape(-1, HEAD_DIM),
             cache_idx_k.reshape(depth * n_pool, PAGE_SIZE, IDX_DIM))
    for l in range(depth):
        ada = _ada(c_all, w_ada[l], b_ada[l]).reshape(c_all.shape[0], N_ADA, d)
        w = _split_w_in(w_in[l], d)
        p = dict(g_norm1=g_norm1[l], g_norm2=g_norm2[l], b_fgate=b_fgate[l], g_qn_a=g_qn_a[l], g_kn_a=g_kn_a[l],
                 g_qn_b=g_qn_b[l], g_kn_b=g_kn_b[l],
                 w_branch_a=w_branch_a[l].astype(BF16), w_branch_b=w_branch_b[l].astype(BF16),
                 w_out=w_out[l].astype(BF16), wq_t=w_peer_q[l].T.astype(BF16),
                 subkeys=peer_subkeys[l].reshape(PEER_HEADS * 2, PEER_KEYS, -1).astype(BF16),
                 u16=peer_u[l].astype(BF16), v16=peer_v[l].astype(BF16))
        xp, f_p = _prompt_group(xp, ada[:bsz], w, p)
        xs, f_s = _sample_group(xs, ada[bsz:n_c], w, p, pools, page_table + l * n_pool)
        rows_p.append(_layer_outputs(f_p, bsz, t))
        rows_s.append(_layer_outputs(f_s, bs, ts))
    outs_p = [jnp.stack([r[i] for r in rows_p], axis=0) for i in range(6)]
    outs_s = [jnp.stack([r[i] for r in rows_s], axis=0) for i in range(6)]
    return (xp, xs, *outs_p, *outs_s)
```

```python
import functools

import jax
import jax.numpy as jnp
from jax import lax
from jax.experimental import pallas as pl
from jax.experimental.pallas import tpu as pltpu

F32 = jnp.float32
BF16 = jnp.bfloat16
I32 = jnp.int32

HEAD_DIM = 128
A_HEADS = 8
A_KV_HEADS = 4
B_HEADS = 8
B_KV_HEADS = 4
IDX_HEADS = 16
IDX_DIM = 64
ROPE_THETA = 500000.0
ROPE_FRAC = 4
TOPK_MAX = 256
PEER_KEYS = 128
PEER_HEADS = 8
PEER_TOPK = 16
PAGE_SIZE = 128
EPS = 1e-6
ATTN_SCALE = HEAD_DIM ** -0.5
N_ADA = 6
LANES = 128
MXU_COLS = 256
NEG = -1e30
LOWEST = -1.0e38
BISECT_ITERS = 40
VMEM_LIMIT = 56 * 1024 * 1024


def _cparams(*sem):
    return pltpu.CompilerParams(dimension_semantics=sem, vmem_limit_bytes=VMEM_LIMIT)


def _tile(n, pref, mult=LANES):
    best = None
    t = mult
    while t <= min(n, pref):
        if n % t == 0:
            best = t
        t += mult
    return best if best is not None else n


def _dot(a, b):
    return jnp.dot(a, b, preferred_element_type=F32)


def _dot_nt(a, b):
    return lax.dot_general(a, b, (((1,), (1,)), ((), ())), preferred_element_type=F32)


def _sigmoid(x):
    return 1.0 / (1.0 + jnp.exp(-x))


def _rep(n, pref):
    return max(r for r in range(1, min(n, pref) + 1) if n % r == 0)


def _ada_kernel(c_ref, w_ref, b_ref, o_ref):
    c = c_ref[...]
    s = c * _sigmoid(c)
    o_ref[...] = _dot(s.astype(BF16), w_ref[...].astype(BF16)) + b_ref[...]


def _ada(c_all, w_ada, b_ada):
    m, d = c_all.shape
    n = w_ada.shape[1]
    tn = _tile(n, 1024)
    return pl.pallas_call(
        _ada_kernel,
        grid=(n // tn,),
        in_specs=[pl.BlockSpec((m, d), lambda j: (0, 0)),
                  pl.BlockSpec((d, tn), lambda j: (0, j)),
                  pl.BlockSpec((1, tn), lambda j: (0, j))],
        out_specs=pl.BlockSpec((m, tn), lambda j: (0, j)),
        out_shape=jax.ShapeDtypeStruct((m, n), F32),
        compiler_params=_cparams("arbitrary"),
        name="ada",
    )(c_all, w_ada, b_ada.reshape(1, n))


def _rms_mod(x, g, sc, sh):
    y = x * lax.rsqrt(jnp.mean(x * x, axis=-1, keepdims=True) + EPS)
    return (y * g) * (1.0 + sc) + sh


def _prenorm_kernel(x_ref, g_ref, sc_ref, sh_ref, o_ref):
    o_ref[...] = _rms_mod(x_ref[...], g_ref[...], sc_ref[...], sh_ref[...]).astype(o_ref.dtype)


def _mod_spec(mod, tm, rows_per_group):
    r, d = mod.shape[1], mod.shape[2]
    return pl.BlockSpec((None, r, d), lambda i: ((i * tm) // rows_per_group, 0, 0))


def _prenorm(x, g, sc, sh, tm, rows_per_group):
    n, d = x.shape
    return pl.pallas_call(
        _prenorm_kernel,
        grid=(n // tm,),
        in_specs=[pl.BlockSpec((tm, d), lambda i: (i, 0)),
                  pl.BlockSpec((1, d), lambda i: (0, 0)),
                  _mod_spec(sc, tm, rows_per_group),
                  _mod_spec(sh, tm, rows_per_group)],
        out_specs=pl.BlockSpec((tm, d), lambda i: (i, 0)),
        out_shape=jax.ShapeDtypeStruct((n, d), BF16),
        compiler_params=_cparams("arbitrary"),
        name="prenorm",
    )(x, g.reshape(1, d), sc, sh)


def _rope_tables(pos, head_dim):
    rd = head_dim // ROPE_FRAC
    half = rd // 2
    inv = ROPE_THETA ** (-jnp.arange(half, dtype=F32) / half)
    ang = pos.astype(F32)[:, None] * inv[None, :]
    cos, sin = jnp.cos(ang), jnp.sin(ang)
    t = pos.shape[0]
    rest = head_dim - rd
    c = jnp.concatenate([cos, cos, jnp.ones((t, rest), F32)], axis=-1)
    sa = jnp.concatenate([-sin, jnp.zeros((t, half + rest), F32)], axis=-1)
    sb = jnp.concatenate([jnp.zeros((t, half), F32), sin, jnp.zeros((t, rest), F32)], axis=-1)
    reps = LANES // head_dim
    return tuple(jnp.tile(a, (1, reps)) for a in (c, sa, sb)), half


def _rope(y, c, sa, sb, half):
    return y * c + pltpu.roll(y, LANES - half, 1) * sa + pltpu.roll(y, half, 1) * sb


def _head_norm(chunk, gain):
    y = chunk * lax.rsqrt(jnp.mean(chunk * chunk, axis=-1, keepdims=True) + EPS)
    return y * gain


def _qkv_kernel(*refs, n_q, n_k, rope_half, with_bf16):
    h_ref, w_ref, gain_ref = refs[:3]
    pos = 3
    if rope_half:
        c_ref, sa_ref, sb_ref = refs[3:6]
        pos = 6
    q_ref, k_ref, v_ref = refs[pos:pos + 3]
    z = _dot(h_ref[...], w_ref[...])
    for ch in range(n_q + n_k):
        sl = slice(ch * LANES, (ch + 1) * LANES)
        y = _head_norm(z[:, sl], gain_ref[:, sl])
        if rope_half:
            y = _rope(y, c_ref[...], sa_ref[...], sb_ref[...], rope_half)
        if ch < n_q:
            q_ref[:, sl] = (y * ATTN_SCALE).astype(q_ref.dtype)
        else:
            k_ref[:, (ch - n_q) * LANES:(ch - n_q + 1) * LANES] = y
    v = z[:, (n_q + n_k) * LANES:]
    v_ref[...] = v
    if with_bf16:
        kb_ref, vb_ref = refs[pos + 3:pos + 5]
        kb_ref[...] = k_ref[...].astype(BF16)
        vb_ref[...] = v.astype(BF16)


def _qkv_proj(h, w, gain, tables, rope_half, tm, n_q, n_k, with_bf16):
    n, d = h.shape
    wq, wk = n_q * LANES, n_k * LANES
    width = w.shape[1]
    row = lambda i: (i, 0)
    const = lambda i: (0, 0)
    in_specs = [pl.BlockSpec((tm, d), row), pl.BlockSpec((d, width), const),
                pl.BlockSpec((1, wq + wk), const)]
    args = [h, w, gain]
    if rope_half:
        nt = tables[0].shape[0] // tm
        for t in tables:
            in_specs.append(pl.BlockSpec((tm, LANES), lambda i: (i % nt, 0)))
            args.append(t)
    out_shape = [jax.ShapeDtypeStruct((n, wq), BF16), jax.ShapeDtypeStruct((n, wk), F32),
                 jax.ShapeDtypeStruct((n, wk), F32)]
    out_specs = [pl.BlockSpec((tm, wq), row), pl.BlockSpec((tm, wk), row), pl.BlockSpec((tm, wk), row)]
    if with_bf16:
        out_shape += [jax.ShapeDtypeStruct((n, wk), BF16)] * 2
        out_specs += [pl.BlockSpec((tm, wk), row)] * 2
    return pl.pallas_call(
        functools.partial(_qkv_kernel, n_q=n_q, n_k=n_k, rope_half=rope_half, with_bf16=with_bf16),
        grid=(n // tm,), in_specs=in_specs, out_specs=out_specs, out_shape=out_shape,
        compiler_params=_cparams("arbitrary"), name="qkv_proj",
    )(*args)


def _qidx_kernel(h_ref, w_ref, c_ref, sa_ref, sb_ref, o_ref, *, half):
    z = _dot(h_ref[...], w_ref[...])
    for ch in range(IDX_HEADS // 2):
        y = _rope(z[:, ch * LANES:(ch + 1) * LANES], c_ref[...], sa_ref[...], sb_ref[...], half)
        o_ref[2 * ch] = y[:, :IDX_DIM].astype(o_ref.dtype)
        o_ref[2 * ch + 1] = y[:, IDX_DIM:].astype(o_ref.dtype)


def _qidx_proj(h, w, tables, half, tm):
    n, d = h.shape
    nt = tables[0].shape[0] // tm
    tab = pl.BlockSpec((tm, LANES), lambda i: (i % nt, 0))
    return pl.pallas_call(
        functools.partial(_qidx_kernel, half=half),
        grid=(n // tm,),
        in_specs=[pl.BlockSpec((tm, d), lambda i: (i, 0)),
                  pl.BlockSpec((d, IDX_HEADS * IDX_DIM), lambda i: (0, 0)), tab, tab, tab],
        out_specs=pl.BlockSpec((IDX_HEADS, tm, IDX_DIM), lambda i: (0, i, 0)),
        out_shape=jax.ShapeDtypeStruct((IDX_HEADS, n, IDX_DIM), BF16),
        compiler_params=_cparams("arbitrary"), name="qidx_proj",
    )(h, w, *tables)


def _misc_kernel(h_ref, w_ref, bf_ref, c_ref, sa_ref, sb_ref, logf_ref, ki_ref, wi_ref, *, half):
    z = _dot(h_ref[...], w_ref[...])
    f = z[:, :A_HEADS] + bf_ref[...]
    logf_ref[...] = jnp.minimum(f, 0.0) - jnp.log1p(jnp.exp(-jnp.abs(f)))
    y = _rope(z[:, LANES:2 * LANES], c_ref[...], sa_ref[...], sb_ref[...], half)
    ki_ref[...] = y[:, :IDX_DIM]
    wi_ref[...] = z[:, 2 * LANES:2 * LANES + IDX_HEADS]


def _misc_proj(h, w, b_fgate, tables, half, tm):
    n, d = h.shape
    nt = tables[0].shape[0] // tm
    tab = pl.BlockSpec((tm, LANES), lambda i: (i % nt, 0))
    row = lambda i: (i, 0)
    return pl.pallas_call(
        functools.partial(_misc_kernel, half=half),
        grid=(n // tm,),
        in_specs=[pl.BlockSpec((tm, d), row), pl.BlockSpec((d, 3 * LANES), lambda i: (0, 0)),
                  pl.BlockSpec((1, A_HEADS), lambda i: (0, 0)), tab, tab, tab],
        out_specs=[pl.BlockSpec((tm, A_HEADS), row), pl.BlockSpec((tm, IDX_DIM), row),
                   pl.BlockSpec((tm, IDX_HEADS), row)],
        out_shape=[jax.ShapeDtypeStruct((n, A_HEADS), F32), jax.ShapeDtypeStruct((n, IDX_DIM), F32),
                   jax.ShapeDtypeStruct((n, IDX_HEADS), F32)],
        compiler_params=_cparams("arbitrary"), name="misc_proj",
    )(h, w, b_fgate.reshape(1, A_HEADS), *tables)


def _gate_kernel(h_ref, w_ref, o_ref):
    o_ref[...] = _sigmoid(_dot(h_ref[...], w_ref[...]))


def _gate_proj(h, w, tm):
    n, d = h.shape
    width = w.shape[1]
    return pl.pallas_call(
        _gate_kernel, grid=(n // tm,),
        in_specs=[pl.BlockSpec((tm, d), lambda i: (i, 0)), pl.BlockSpec((d, width), lambda i: (0, 0))],
        out_specs=pl.BlockSpec((tm, width), lambda i: (i, 0)),
        out_shape=jax.ShapeDtypeStruct((n, width), F32),
        compiler_params=_cparams("arbitrary"), name="gate_proj",
    )(h, w)


def _cumsum_kernel(x_ref, o_ref, *, tc):
    t = x_ref.shape[-1]
    r = lax.broadcasted_iota(I32, (tc, tc), 0)
    c = lax.broadcasted_iota(I32, (tc, tc), 1)
    tri = jnp.where(r <= c, 1.0, 0.0).astype(F32)
    carry = jnp.zeros((x_ref.shape[0], 1), F32)
    for blk in range(t // tc):
        sl = slice(blk * tc, (blk + 1) * tc)
        cs = jnp.dot(x_ref[:, sl], tri, precision=lax.Precision.HIGHEST, preferred_element_type=F32) + carry
        o_ref[:, sl] = cs
        carry = cs[:, tc - 1:tc]


def _cumsum_lanes(x):
    b, h, t = x.shape
    tc = _tile(t, 512)
    return pl.pallas_call(
        functools.partial(_cumsum_kernel, tc=tc), grid=(b,),
        in_specs=[pl.BlockSpec((None, h, t), lambda i: (i, 0, 0))],
        out_specs=pl.BlockSpec((None, h, t), lambda i: (i, 0, 0)),
        out_shape=jax.ShapeDtypeStruct((b, h, t), F32),
        compiler_params=_cparams("arbitrary"), name="cumsum",
    )(x)


def _online_update(s, v16, m_ref, l_ref, acc_ref):
    m_prev = m_ref[...]
    m_new = jnp.maximum(m_prev, jnp.max(s, axis=-1, keepdims=True))
    alpha = jnp.exp(m_prev - m_new)
    p = jnp.exp(s - m_new)
    l_ref[...] = alpha * l_ref[...] + jnp.sum(p, axis=-1, keepdims=True)
    acc_ref[...] = alpha * acc_ref[...] + _dot(p.astype(BF16), v16)
    m_ref[...] = m_new


def _fox_kernel(q_ref, k_ref, v_ref, fq_ref, fk_ref, o_ref, m_sc, l_sc, acc_sc, *, tq, tk, group):
    i = pl.program_id(1)
    kk = pl.program_id(2)
    n_heads = q_ref.shape[1] // LANES

    @pl.when(kk == 0)
    def _():
        m_sc[...] = jnp.full(m_sc.shape, NEG, F32)
        l_sc[...] = jnp.zeros(l_sc.shape, F32)
        acc_sc[...] = jnp.zeros(acc_sc.shape, F32)

    @pl.when(kk * tk <= i * tq + tq - 1)
    def _():
        row = i * tq + lax.broadcasted_iota(I32, (tq, tk), 0)
        col = kk * tk + lax.broadcasted_iota(I32, (tq, tk), 1)
        causal = col <= row
        for h in range(n_heads):
            kv = slice((h // group) * LANES, (h // group + 1) * LANES)
            s = _dot_nt(q_ref[:, h * LANES:(h + 1) * LANES], k_ref[:, kv].astype(BF16))
            s = s + fq_ref[:, h:h + 1] - fk_ref[h:h + 1, :]
            s = jnp.where(causal, s, NEG)
            _online_update(s, v_ref[:, kv].astype(BF16), m_sc.at[h], l_sc.at[h], acc_sc.at[h])

    @pl.when(kk == pl.num_programs(2) - 1)
    def _():
        for h in range(n_heads):
            o_ref[:, h * LANES:(h + 1) * LANES] = (acc_sc[h] / l_sc[h]).astype(o_ref.dtype)


def _fox_prompt(q, k, v, fq, fk, bsz, t):
    tq = _tile(t, 256)
    tk = _tile(t, 1024)
    nq, nk = t // tq, t // tk
    kvw = A_KV_HEADS * LANES
    last_k = lambda i, kk: jnp.minimum(kk, (i * tq + tq - 1) // tk)
    return pl.pallas_call(
        functools.partial(_fox_kernel, tq=tq, tk=tk, group=A_HEADS // A_KV_HEADS),
        grid=(bsz, nq, nk),
        in_specs=[pl.BlockSpec((tq, A_HEADS * LANES), lambda b, i, kk: (b * nq + i, 0)),
                  pl.BlockSpec((tk, kvw), lambda b, i, kk: (b * nk + last_k(i, kk), 0)),
                  pl.BlockSpec((tk, kvw), lambda b, i, kk: (b * nk + last_k(i, kk), 0)),
                  pl.BlockSpec((None, tq, A_HEADS), lambda b, i, kk: (b, i, 0)),
                  pl.BlockSpec((None, A_HEADS, tk), lambda b, i, kk: (b, 0, last_k(i, kk)))],
        out_specs=pl.BlockSpec((tq, A_HEADS * LANES), lambda b, i, kk: (b * nq + i, 0)),
        out_shape=jax.ShapeDtypeStruct((bsz * t, A_HEADS * LANES), BF16),
        scratch_shapes=[pltpu.VMEM((A_HEADS, tq, 1), F32), pltpu.VMEM((A_HEADS, tq, 1), F32),
                        pltpu.VMEM((A_HEADS, tq, LANES), F32)],
        compiler_params=_cparams("arbitrary", "arbitrary", "arbitrary"),
        name="fox_prompt",
    )(q, k, v, fq, fk)


def _kth_value(count_ge, lo, hi, c_lo, k):
    def cond(st):
        return jnp.logical_and(st[0] < BISECT_ITERS, jnp.max(st[3]) > k)

    def body(st):
        it, lo, hi, c_lo = st
        mid = 0.5 * lo + 0.5 * hi
        c = count_ge(mid)
        ge = c >= k
        return it + 1, jnp.where(ge, mid, lo), jnp.where(ge, hi, mid), jnp.where(ge, c, c_lo)

    return lax.while_loop(cond, body, (jnp.int32(0), lo, hi, c_lo))[1]


def _dsa_kernel(qb_ref, qi_ref, wi_ref, ki_ref, kb_ref, vb_ref, o_ref, key_sc, m_sc, l_sc, acc_sc,
                *, tq, tk, topk):
    i = pl.program_id(1)
    nch = (i * tq + tq + tk - 1) // tk
    row = i * tq + lax.broadcasted_iota(I32, (tq, 1), 0)

    def scores(c, carry):
        mx, mn = carry
        kc = ki_ref[pl.ds(pl.multiple_of(c * tk, tk), tk), :].astype(BF16)
        acc = jnp.zeros((tq, tk), F32)
        for h in range(IDX_HEADS):
            acc = acc + wi_ref[:, h:h + 1] * jnp.maximum(_dot_nt(qi_ref[h], kc), 0.0)
        valid = (c * tk + lax.broadcasted_iota(I32, (1, tk), 1)) <= row
        key_sc[c] = jnp.where(valid, acc, -jnp.inf)
        mx = jnp.maximum(mx, jnp.max(jnp.where(valid, acc, -jnp.inf), axis=-1, keepdims=True))
        mn = jnp.minimum(mn, jnp.min(jnp.where(valid, acc, jnp.inf), axis=-1, keepdims=True))
        return mx, mn

    mx, mn = lax.fori_loop(0, nch, scores, (jnp.full((tq, 1), -jnp.inf, F32), jnp.full((tq, 1), jnp.inf, F32)))

    def count_ge(cand):
        def cbody(c, cnt):
            ge = jnp.where(key_sc[c] >= cand, 1.0, 0.0)
            for s in range(tk // LANES):
                cnt = cnt + ge[:, s * LANES:(s + 1) * LANES]
            return cnt
        cnt = lax.fori_loop(0, nch, cbody, jnp.zeros((tq, LANES), F32))
        return jnp.sum(cnt, axis=-1, keepdims=True)

    many = row >= topk
    lo0 = jnp.where(many, mn, LOWEST)
    hi0 = jnp.where(many, mx, LOWEST)
    c0 = jnp.where(many, (row + 1).astype(F32), 0.0)
    thr = _kth_value(count_ge, lo0, hi0, c0, float(topk))

    m_sc[...] = jnp.full(m_sc.shape, NEG, F32)
    l_sc[...] = jnp.zeros(l_sc.shape, F32)
    acc_sc[...] = jnp.zeros(acc_sc.shape, F32)

    def attend(c, carry):
        bias = jnp.where(key_sc[c] >= thr, 0.0, NEG)
        rows = pl.ds(pl.multiple_of(c * tk, tk), tk)
        for h in range(B_HEADS):
            j = h // (B_HEADS // B_KV_HEADS)
            k16 = kb_ref[rows, j * LANES:(j + 1) * LANES]
            v16 = vb_ref[rows, j * LANES:(j + 1) * LANES]
            s = _dot_nt(qb_ref[:, h * LANES:(h + 1) * LANES], k16) + bias
            _online_update(s, v16, m_sc.at[h], l_sc.at[h], acc_sc.at[h])
        return carry

    lax.fori_loop(0, nch, attend, 0)
    for h in range(B_HEADS):
        o_ref[:, h * LANES:(h + 1) * LANES] = (acc_sc[h] / l_sc[h]).astype(o_ref.dtype)


def _dsa_prompt(qb, qi, wi, ki, kb16, vb16, bsz, t):
    tq = _tile(t, 256)
    tk = _tile(t, 1024)
    nq = t // tq
    topk = min(TOPK_MAX, t // 4)
    kvw = B_KV_HEADS * LANES
    return pl.pallas_call(
        functools.partial(_dsa_kernel, tq=tq, tk=tk, topk=topk),
        grid=(bsz, nq),
        in_specs=[pl.BlockSpec((tq, B_HEADS * LANES), lambda b, i: (b * nq + i, 0)),
                  pl.BlockSpec((IDX_HEADS, tq, IDX_DIM), lambda b, i: (0, b * nq + i, 0)),
                  pl.BlockSpec((tq, IDX_HEADS), lambda b, i: (b * nq + i, 0)),
                  pl.BlockSpec((t, IDX_DIM), lambda b, i: (b, 0)),
                  pl.BlockSpec((t, kvw), lambda b, i: (b, 0)),
                  pl.BlockSpec((t, kvw), lambda b, i: (b, 0))],
        out_specs=pl.BlockSpec((tq, B_HEADS * LANES), lambda b, i: (b * nq + i, 0)),
        out_shape=jax.ShapeDtypeStruct((bsz * t, B_HEADS * LANES), BF16),
        scratch_shapes=[pltpu.VMEM((t // tk, tq, tk), F32),
                        pltpu.VMEM((B_HEADS, tq, 1), F32), pltpu.VMEM((B_HEADS, tq, 1), F32),
                        pltpu.VMEM((B_HEADS, tq, LANES), F32)],
        compiler_params=_cparams("arbitrary", "arbitrary"),
        name="dsa_prompt",
    )(qb, qi, wi, ki, kb16, vb16)


def _page_specs(block, n_rep):
    def spec(r):
        return pl.BlockSpec(block, lambda b, pg, pt: (pt[b, pg * n_rep + r],) + (0,) * (len(block) - 1))
    return [spec(r) for r in range(n_rep)]


def _spread_matrix(n_kv, lower_tri):
    r = lax.broadcasted_iota(I32, (PAGE_SIZE, PAGE_SIZE * n_kv), 0)
    key = lax.shift_right_logical(lax.broadcasted_iota(I32, (PAGE_SIZE, PAGE_SIZE * n_kv), 1),
                                  n_kv.bit_length() - 1)
    return jnp.where((r <= key) if lower_tri else (r == key), 1.0, 0.0)


def _fscan_kernel(pt_ref, *refs, n_rep, n_kv):
    pages = refs[:n_rep]
    new_ref, fp_ref, fn_ref, carry_sc = refs[n_rep:]
    pg = pl.program_id(1)
    h = new_ref.shape[0]
    wide = PAGE_SIZE * n_kv
    m = n_rep * h
    hp = lax.Precision.HIGHEST

    @pl.when(pg == 0)
    def _():
        carry_sc[...] = jnp.zeros(carry_sc.shape, F32)

    x = jnp.concatenate([pages[rr][...] for rr in range(n_rep)], axis=0)
    cs = jnp.dot(x, _spread_matrix(n_kv, True), precision=hp, preferred_element_type=F32)
    tot = cs[:, wide - 1:wide]
    i = lax.broadcasted_iota(I32, (m, m), 0)
    j = lax.broadcasted_iota(I32, (m, m), 1)
    sh = h.bit_length() - 1
    earlier_page = jnp.where(lax.shift_right_logical(j, sh) < lax.shift_right_logical(i, sh), 1.0, 0.0)
    lower = jnp.where((i & (h - 1)) == (j & (h - 1)), earlier_page, 0.0)
    pref = jnp.dot(lower, jnp.broadcast_to(tot, (m, LANES)), precision=hp, preferred_element_type=F32)[:, :1]
    carry = carry_sc[...]
    out = cs + (pref + jnp.concatenate([carry] * n_rep, axis=0))
    for rr in range(n_rep):
        fp_ref[:, rr * wide:(rr + 1) * wide] = -out[rr * h:(rr + 1) * h, :]
    carry = carry + pref[m - h:, :] + tot[m - h:, :]
    carry_sc[...] = carry

    @pl.when(pg == pl.num_programs(1) - 1)
    def _():
        r = lax.broadcasted_iota(I32, (PAGE_SIZE, PAGE_SIZE), 0)
        c = lax.broadcasted_iota(I32, (PAGE_SIZE, PAGE_SIZE), 1)
        tri = jnp.where(r <= c, 1.0, 0.0)
        fn_ref[...] = jnp.dot(new_ref[...], tri, precision=hp, preferred_element_type=F32) + carry


def _fscan(page_table, logf_pool_t, logf_new_t, n_rep, n_kv):
    bsz, n_pages = page_table.shape
    h = logf_pool_t.shape[1]
    wide = PAGE_SIZE * n_kv
    grid_spec = pltpu.PrefetchScalarGridSpec(
        num_scalar_prefetch=1, grid=(bsz, n_pages // n_rep),
        in_specs=_page_specs((None, h, PAGE_SIZE), n_rep)
        + [pl.BlockSpec((None, h, PAGE_SIZE), lambda b, pg, pt: (b, 0, 0))],
        out_specs=[pl.BlockSpec((None, h, n_rep * wide), lambda b, pg, pt: (b, 0, pg)),
                   pl.BlockSpec((None, h, PAGE_SIZE), lambda b, pg, pt: (b, 0, 0))],
        scratch_shapes=[pltpu.VMEM((h, 1), F32)])
    return pl.pallas_call(
        functools.partial(_fscan_kernel, n_rep=n_rep, n_kv=n_kv), grid_spec=grid_spec,
        out_shape=[jax.ShapeDtypeStruct((bsz, h, n_pages * wide), F32),
                   jax.ShapeDtypeStruct((bsz, h, PAGE_SIZE), F32)],
        compiler_params=_cparams("arbitrary", "arbitrary"), name="fscan",
    )(page_table, *([logf_pool_t] * n_rep), logf_new_t)


def _sidx_kernel(pt_ref, *refs, n_rep, n_tok, topk, n_kv):
    pages = refs[:n_rep]
    qi_ref, wi_ref, kn_ref, kb_ref, nb_ref, sc_sc = refs[n_rep:]
    pg = pl.program_id(1)
    npg = pl.num_programs(1)
    rows = 8
    pad = jnp.zeros((rows - n_tok, PAGE_SIZE), F32)

    def score(keys_f32):
        d = jnp.maximum(_dot_nt(qi_ref[...], keys_f32.astype(BF16)), 0.0) * wi_ref[...]
        return jnp.concatenate([jnp.sum(d.reshape(n_tok, IDX_HEADS, PAGE_SIZE), axis=1), pad], axis=0)

    sc_sc[pg] = jnp.concatenate([score(pages[rr][...]) for rr in range(n_rep)], axis=1)

    @pl.when(pg == npg - 1)
    def _():
        t_idx = lax.broadcasted_iota(I32, (rows, PAGE_SIZE), 0)
        s_idx = lax.broadcasted_iota(I32, (rows, PAGE_SIZE), 1)
        new_valid = (s_idx <= t_idx) & (t_idx < n_tok)
        new_raw = score(kn_ref[...])
        new = jnp.where(new_valid, new_raw, -jnp.inf)
        past = sc_sc[...]

        def count_ge(cand):
            c_past = jnp.sum(jnp.sum(jnp.where(past >= cand[None], 1.0, 0.0), axis=0), axis=-1, keepdims=True)
            return c_past + jnp.sum(jnp.where(new >= cand, 1.0, 0.0), axis=-1, keepdims=True)

        mx = jnp.maximum(jnp.max(jnp.max(past, axis=0), axis=-1, keepdims=True),
                         jnp.max(new, axis=-1, keepdims=True))
        mn = jnp.minimum(jnp.min(jnp.min(past, axis=0), axis=-1, keepdims=True),
                         jnp.min(jnp.where(new_valid, new_raw, jnp.inf), axis=-1, keepdims=True))
        tok = lax.broadcasted_iota(I32, (rows, 1), 0)
        n_valid = jnp.where(tok < n_tok, (past.shape[0] * past.shape[2] + 1 + tok).astype(F32), 0.0)
        many = n_valid > topk
        thr = _kth_value(count_ge, jnp.where(many, mn, LOWEST), jnp.where(many, mx, LOWEST),
                         jnp.where(many, n_valid, 0.0), float(topk))
        spread = _spread_matrix(n_kv, False).astype(BF16)
        wide = PAGE_SIZE * n_kv
        for s in range(past.shape[0]):
            sel = jnp.where(past[s] >= thr, 1.0, 0.0)
            stacked = jnp.concatenate([sel[:, c * PAGE_SIZE:(c + 1) * PAGE_SIZE] for c in range(n_rep)], axis=0)
            ex = _dot(stacked.astype(BF16), spread)
            for c in range(n_rep):
                kb_ref[:, (s * n_rep + c) * wide:(s * n_rep + c + 1) * wide] = jnp.where(
                    ex[c * rows:(c + 1) * rows, :] > 0.5, 0.0, NEG)
        nb_ref[...] = jnp.where(new >= thr, 0.0, NEG)


def _sample_index(page_table, idx_pool, qi, wi, ki_new, n_rep, n_tok, n_kv):
    bsz, n_pages = page_table.shape
    p = n_pages * PAGE_SIZE
    topk = min(TOPK_MAX, (p + n_tok) // 4)
    nr = n_tok * IDX_HEADS
    steps = n_pages // n_rep
    wide = n_rep * PAGE_SIZE
    grid_spec = pltpu.PrefetchScalarGridSpec(
        num_scalar_prefetch=1, grid=(bsz, steps),
        in_specs=_page_specs((None, PAGE_SIZE, IDX_DIM), n_rep)
        + [pl.BlockSpec((None, nr, IDX_DIM), lambda b, pg, pt: (b, 0, 0)),
           pl.BlockSpec((None, nr, 1), lambda b, pg, pt: (b, 0, 0)),
           pl.BlockSpec((None, PAGE_SIZE, IDX_DIM), lambda b, pg, pt: (b, 0, 0))],
        out_specs=[pl.BlockSpec((None, 8, p * n_kv), lambda b, pg, pt: (b, 0, 0)),
                   pl.BlockSpec((None, 8, PAGE_SIZE), lambda b, pg, pt: (b, 0, 0))],
        scratch_shapes=[pltpu.VMEM((steps, 8, wide), F32)])
    return pl.pallas_call(
        functools.partial(_sidx_kernel, n_rep=n_rep, n_tok=n_tok, topk=topk, n_kv=n_kv), grid_spec=grid_spec,
        out_shape=[jax.ShapeDtypeStruct((bsz, 8, p * n_kv), F32),
                   jax.ShapeDtypeStruct((bsz, 8, PAGE_SIZE), F32)],
        compiler_params=_cparams("arbitrary", "arbitrary"), name="sample_index",
    )(page_table, *([idx_pool] * n_rep), qi, wi, ki_new)


def _paged_attn_kernel(pt_ref, *refs, n_rep, n_kv, n_heads, per_head):
    kp = refs[:n_rep]
    vp = refs[n_rep:2 * n_rep]
    q_ref, kb_ref, rb_ref, kn_ref, vn_ref, nb_ref, o_ref, m_sc, l_sc, acc_sc = refs[2 * n_rep:]
    pg = pl.program_id(1)
    nrow = q_ref.shape[0]
    wide = PAGE_SIZE * n_kv

    @pl.when(pg == 0)
    def _():
        m_sc[...] = jnp.full(m_sc.shape, NEG, F32)
        l_sc[...] = jnp.zeros(l_sc.shape, F32)
        acc_sc[...] = jnp.zeros(acc_sc.shape, F32)

    def expand(kb):
        if per_head:
            return jnp.concatenate([kb] * (nrow // kb.shape[0]), axis=0)
        return jnp.concatenate([jnp.broadcast_to(kb[t:t + 1], (8, kb.shape[1])) for t in range(nrow // 8)], axis=0)

    head = lax.broadcasted_iota(I32, (nrow, wide), 0) % n_heads
    kv_col = lax.broadcasted_iota(I32, (nrow, wide), 1) % n_kv
    head_bias = jnp.where(head // (n_heads // n_kv) == kv_col, 0.0, NEG)
    q = q_ref[...]
    rb = rb_ref[...]

    def update(s, vs, width):
        m_prev = m_sc[...]
        m_new = jnp.maximum(m_prev, jnp.max(s, axis=-1, keepdims=True))
        alpha = jnp.exp(m_prev - m_new)
        p = jnp.exp(s - m_new)
        l_sc[...] = alpha * l_sc[...] + jnp.sum(p, axis=-1, keepdims=True)
        acc = alpha * acc_sc[...]
        for r, v in enumerate(vs):
            acc = acc + _dot(p[:, r * width:(r + 1) * width].astype(BF16), v.astype(BF16))
        acc_sc[...] = acc
        m_sc[...] = m_new

    s = jnp.concatenate([_dot_nt(q, kp[r][...].astype(BF16)) for r in range(n_rep)], axis=1)
    bias = expand(kb_ref[...]) + jnp.concatenate([head_bias] * n_rep, axis=1)
    update(s + rb + bias, [vp[r][...] for r in range(n_rep)], wide)

    @pl.when(pg == pl.num_programs(1) - 1)
    def _():
        sn = _dot_nt(q, kn_ref[...].astype(BF16)) + rb + nb_ref[...]
        update(sn, [vn_ref[...]], PAGE_SIZE)
        o_ref[...] = acc_sc[...] / l_sc[...]


def _paged_attn(page_table, k_pool, v_pool, q, key_bias, row_bias, k_new, v_new, new_bias, n_rep, n_kv, n_heads,
                per_head):
    bsz, n_pages = page_table.shape
    nrow, w = q.shape[1], q.shape[2]
    kbh = key_bias.shape[1]
    wide = PAGE_SIZE * n_kv
    const = lambda b, pg, pt: (b, 0, 0)
    grid_spec = pltpu.PrefetchScalarGridSpec(
        num_scalar_prefetch=1, grid=(bsz, n_pages // n_rep),
        in_specs=_page_specs((wide, w), n_rep) + _page_specs((wide, w), n_rep)
        + [pl.BlockSpec((None, nrow, w), const),
           pl.BlockSpec((None, kbh, n_rep * wide), lambda b, pg, pt: (b, 0, pg)),
           pl.BlockSpec((None, nrow, 1), const),
           pl.BlockSpec((None, PAGE_SIZE, w), const),
           pl.BlockSpec((None, PAGE_SIZE, w), const),
           pl.BlockSpec((None, nrow, PAGE_SIZE), const)],
        out_specs=pl.BlockSpec((None, nrow, w), const),
        scratch_shapes=[pltpu.VMEM((nrow, 1), F32), pltpu.VMEM((nrow, 1), F32), pltpu.VMEM((nrow, w), F32)])
    return pl.pallas_call(
        functools.partial(_paged_attn_kernel, n_rep=n_rep, n_kv=n_kv, n_heads=n_heads, per_head=per_head),
        grid_spec=grid_spec,
        out_shape=jax.ShapeDtypeStruct((bsz, nrow, w), F32),
        compiler_params=_cparams("arbitrary", "arbitrary"), name="paged_attn",
    )(page_table, *([k_pool] * n_rep), *([v_pool] * n_rep), q, key_bias, row_bias, k_new, v_new, new_bias)


def _merge_kernel(oa_ref, ob_ref, ga_ref, gb_ref, wa_ref, wb_ref, o_ref):
    m = ga_ref[...] * _dot(oa_ref[...], wa_ref[...]) + gb_ref[...] * _dot(ob_ref[...], wb_ref[...])
    o_ref[...] = m.astype(o_ref.dtype)


def _merge(oa, ob, ga, gb, wa, wb, tm):
    n, e = oa.shape
    d = wa.shape[1]
    row = lambda i: (i, 0)
    const = lambda i: (0, 0)
    return pl.pallas_call(
        _merge_kernel, grid=(n // tm,),
        in_specs=[pl.BlockSpec((tm, e), row), pl.BlockSpec((tm, e), row),
                  pl.BlockSpec((tm, d), row), pl.BlockSpec((tm, d), row),
                  pl.BlockSpec((e, d), const), pl.BlockSpec((e, d), const)],
        out_specs=pl.BlockSpec((tm, d), row),
        out_shape=jax.ShapeDtypeStruct((n, d), BF16),
        compiler_params=_cparams("arbitrary"), name="merge",
    )(oa, ob, ga, gb, wa, wb)


def _outproj_kernel(x_ref, m_ref, w_ref, g1_ref, gn_ref, sc_ref, sh_ref, x1_ref, h2t_ref):
    x1 = x_ref[...] + g1_ref[...] * _dot(m_ref[...], w_ref[...])
    x1_ref[...] = x1
    h2t_ref[...] = _rms_mod(x1, gn_ref[...], sc_ref[...], sh_ref[...]).T.astype(h2t_ref.dtype)


def _outproj(x, merged, w_out, ga1, g_norm2, sc2, sh2, tm, rows_per_group):
    n, d = x.shape
    row = lambda i: (i, 0)
    return pl.pallas_call(
        _outproj_kernel, grid=(n // tm,),
        in_specs=[pl.BlockSpec((tm, d), row), pl.BlockSpec((tm, d), row),
                  pl.BlockSpec((d, d), lambda i: (0, 0)),
                  _mod_spec(ga1, tm, rows_per_group),
                  pl.BlockSpec((1, d), lambda i: (0, 0)),
                  _mod_spec(sc2, tm, rows_per_group), _mod_spec(sh2, tm, rows_per_group)],
        out_specs=[pl.BlockSpec((tm, d), row), pl.BlockSpec((d, tm), lambda i: (0, i))],
        out_shape=[jax.ShapeDtypeStruct((n, d), F32), jax.ShapeDtypeStruct((d, n), BF16)],
        compiler_params=_cparams("arbitrary"), name="outproj",
    )(x, merged, w_out, ga1, g_norm2.reshape(1, d), sc2, sh2)


def _top_values(x, n):
    out = []
    for _ in range(n):
        m = jnp.max(x, axis=0, keepdims=True)
        out.append(m)
        x = jnp.where(x == m, -jnp.inf, x)
    return out


def _peer_query_kernel(ht_ref, wq_ref, sk_ref, s_ref, thr_ref, m1_ref, m2_ref, rz_ref, sv_sc):
    qt = _dot(wq_ref[...], ht_ref[...])
    n_hc = sk_ref.shape[0]
    for hc in range(n_hc):
        s = _dot(sk_ref[hc], qt[hc * PEER_KEYS:(hc + 1) * PEER_KEYS, :].astype(BF16))
        s_ref[hc] = s
        for r, m in enumerate(_top_values(s, PEER_TOPK)):
            sv_sc[hc, r:r + 1, :] = m
    for h in range(n_hc // 2):
        sv1 = sv_sc[2 * h]
        sv2 = sv_sc[2 * h + 1]
        cands = [sv1[0:1] + sv2]
        cands += [sv1[a:a + 1] + sv2[0:8] for a in range(1, 8)]
        cands += [sv1[8:16] + sv2[0:1]]
        tops = _top_values(jnp.concatenate(cands, axis=0), PEER_TOPK)
        z = jnp.zeros_like(tops[0])
        for m in tops:
            z = z + jnp.exp(m - tops[0])
        thr_ref[h:h + 1, :] = tops[-1]
        m1_ref[h:h + 1, :] = sv1[0:1]
        m2_ref[h:h + 1, :] = sv2[0:1]
        rz_ref[h:h + 1, :] = 1.0 / z


def _peer_query(h2t, wqt, subkeys, tm):
    d, n = h2t.shape
    n_hc = subkeys.shape[0]
    col = lambda i: (0, i)
    small = jax.ShapeDtypeStruct((PEER_HEADS, n), F32)
    small_spec = pl.BlockSpec((PEER_HEADS, tm), col)
    return pl.pallas_call(
        _peer_query_kernel, grid=(n // tm,),
        in_specs=[pl.BlockSpec((d, tm), col), pl.BlockSpec(wqt.shape, lambda i: (0, 0)),
                  pl.BlockSpec(subkeys.shape, lambda i: (0, 0, 0))],
        out_specs=[pl.BlockSpec((n_hc, PEER_KEYS, tm), lambda i: (0, 0, i))] + [small_spec] * 4,
        out_shape=[jax.ShapeDtypeStruct((n_hc, PEER_KEYS, n), F32)] + [small] * 4,
        scratch_shapes=[pltpu.VMEM((n_hc, PEER_TOPK, tm), F32)],
        compiler_params=_cparams("arbitrary"), name="peer_query",
    )(h2t, wqt, subkeys)


def _peer_weighted_acts(tile, live, a_ref, wa_ref, s_ref, thr_ref, m1_ref, e2_sc):
    ts, tm = a_ref.shape
    half = PEER_KEYS // 2

    def block(q, l):
        def run(after):
            i1 = jnp.clip(tile * (ts // PEER_KEYS) + q, 0, PEER_KEYS - 1)
            lanes = slice(l * LANES, (l + 1) * LANES)
            s1 = [s_ref[2 * h, pl.ds(i1, 1), :][:, lanes] for h in range(PEER_HEADS)]
            e1 = [jnp.exp(s1[h] - m1_ref[h:h + 1, lanes]) for h in range(PEER_HEADS)]
            thr = [jnp.where(live, thr_ref[h:h + 1, lanes], jnp.inf) + after for h in range(PEER_HEADS)]
            halves = []
            for r in range(2):
                rows = slice(r * half, (r + 1) * half)
                w = jnp.zeros((half, LANES), F32)
                for h in range(PEER_HEADS):
                    sel = (s_ref[2 * h + 1, rows, lanes] + s1[h]) >= thr[h]
                    w = w + jnp.where(sel, e2_sc[h, rows, lanes] * e1[h], 0.0)
                a = a_ref[q * PEER_KEYS + r * half:q * PEER_KEYS + (r + 1) * half, lanes]
                halves.append(w * (0.5 * a * (1.0 + lax.erf(a * (2.0 ** -0.5)))))
            blk = jnp.concatenate(halves, axis=0)
            wa_ref[lanes, q * PEER_KEYS:(q + 1) * PEER_KEYS] = blk.T.astype(wa_ref.dtype)
        return run

    return [block(q, l) for q in range(ts // PEER_KEYS) for l in range(tm // LANES)]


def _interleave(vpu_tasks, mxu_tasks):
    n_v, n_m = len(vpu_tasks), len(mxu_tasks)
    done_m = 0
    after = jnp.zeros((1, LANES), F32)
    for i, task in enumerate(vpu_tasks):
        while done_m * n_v < (i + 1) * n_m:
            row = mxu_tasks[done_m]()
            after = jnp.where(row != row, row, 0.0)
            done_m += 1
        task(after)


def _peer_dense_kernel(ht_ref, u_ref, v_ref, s_ref, thr_ref, m1_ref, m2_ref, rz_ref, x1_ref, g2_ref, o_ref,
                       a0, a1, wa0, wa1, e2_sc):
    k = pl.program_id(1)
    last = pl.num_programs(1) - 1
    ts = a0.shape[0]

    @pl.when(k == 0)
    def _():
        o_ref[...] = jnp.zeros(o_ref.shape, F32)
        wa0[...] = jnp.zeros(wa0.shape, wa0.dtype)
        a1[...] = jnp.zeros(a1.shape, F32)
        for h in range(PEER_HEADS):
            e2_sc[h] = jnp.exp(s_ref[2 * h + 1] - m2_ref[h:h + 1, :]) * rz_ref[h:h + 1, :]

    d = o_ref.shape[1]
    cw = min(d, MXU_COLS)
    rh = ts // 2

    def act_chunk(a_ref, base, c):
        def run():
            r = _dot(u_ref[base + c * rh:base + (c + 1) * rh, :], ht_ref[...])
            a_ref[c * rh:(c + 1) * rh, :] = r
            return r[rh - 1:rh, :LANES]
        return run

    def val_chunk(wa_ref, base, c):
        def run():
            r = _dot(wa_ref[...], v_ref[base:base + ts, c * cw:(c + 1) * cw])
            o_ref[:, c * cw:(c + 1) * cw] += r
            return r[r.shape[0] - 1:, :LANES]
        return run

    weights = lambda tile, live, a_ref, wa_ref: _peer_weighted_acts(tile, live, a_ref, wa_ref, s_ref, thr_ref,
                                                                    m1_ref, e2_sc)
    _interleave(weights(2 * k - 1, k >= 1, a1, wa1),
                [act_chunk(a0, 0, c) for c in range(2)] + [val_chunk(wa0, 0, c) for c in range(d // cw)])
    _interleave(weights(2 * k, k < last, a0, wa0),
                [act_chunk(a1, ts, c) for c in range(2)] + [val_chunk(wa1, ts, c) for c in range(d // cw)])

    @pl.when(k == last)
    def _():
        o_ref[...] = x1_ref[...] + g2_ref[...] * o_ref[...]


def _peer_dense(h2t, u16, v16, s_t, thr, m1, m2, rz, x1, ga2, tm, ts, rows_per_group):
    d, n = h2t.shape
    n_exp = u16.shape[0]
    n_hc = s_t.shape[0]
    n_k = n_exp // (2 * ts)
    tok = lambda i, k: (0, i)
    small_spec = pl.BlockSpec((PEER_HEADS, tm), tok)
    g2_spec = pl.BlockSpec((None,) + ga2.shape[1:], lambda i, k: ((i * tm) // rows_per_group, 0, 0))
    return pl.pallas_call(
        _peer_dense_kernel,
        grid=(n // tm, n_k + 1),
        in_specs=[pl.BlockSpec((d, tm), tok),
                  pl.BlockSpec((2 * ts, d), lambda i, k: (jnp.minimum(k, n_k - 1), 0)),
                  pl.BlockSpec((2 * ts, d), lambda i, k: (jnp.maximum(k - 1, 0), 0)),
                  pl.BlockSpec((n_hc, PEER_KEYS, tm), lambda i, k: (0, 0, i))] + [small_spec] * 4
        + [pl.BlockSpec((tm, d), lambda i, k: (i, 0)), g2_spec],
        out_specs=pl.BlockSpec((tm, d), lambda i, k: (i, 0)),
        out_shape=jax.ShapeDtypeStruct((n, d), F32),
        scratch_shapes=[pltpu.VMEM((ts, tm), F32), pltpu.VMEM((ts, tm), F32),
                        pltpu.VMEM((tm, ts), BF16), pltpu.VMEM((tm, ts), BF16),
                        pltpu.VMEM((PEER_HEADS, PEER_KEYS, tm), F32)],
        compiler_params=_cparams("arbitrary", "arbitrary"), name="peer_dense",
    )(h2t, u16, v16, s_t, thr, m1, m2, rz, x1, ga2)


def _split_w_in(w_in, d_model):
    sizes = (A_HEADS * HEAD_DIM, A_KV_HEADS * HEAD_DIM, A_KV_HEADS * HEAD_DIM, A_HEADS,
             B_HEADS * HEAD_DIM, B_KV_HEADS * HEAD_DIM, B_KV_HEADS * HEAD_DIM,
             IDX_HEADS * IDX_DIM, IDX_DIM, IDX_HEADS, d_model, d_model)
    offs = [0]
    for s in sizes:
        offs.append(offs[-1] + s)
    cols = lambda a, b: w_in[:, offs[a]:offs[b]].astype(BF16)
    padw = lambda a: jnp.pad(w_in[:, offs[a]:offs[a + 1]], ((0, 0), (0, LANES - sizes[a]))).astype(BF16)
    return dict(a=cols(0, 3), b=cols(4, 7), qi=cols(7, 8),
                misc=jnp.concatenate([padw(3), padw(8), padw(9)], axis=1),
                gate_a=cols(10, 11), gate_b=cols(11, 12))


def _front(x2, mods, pos, w, p, tm, rows_per_group):
    sh1, sc1 = mods[0], mods[1]
    h = _prenorm(x2, p["g_norm1"], sc1, sh1, tm, rows_per_group)
    tab128, half128 = _rope_tables(pos, HEAD_DIM)
    tab64, half64 = _rope_tables(pos, IDX_DIM)
    gain_a = jnp.concatenate([jnp.tile(p["g_qn_a"], A_HEADS), jnp.tile(p["g_kn_a"], A_KV_HEADS)]).reshape(1, -1)
    gain_b = jnp.concatenate([jnp.tile(p["g_qn_b"], B_HEADS), jnp.tile(p["g_kn_b"], B_KV_HEADS)]).reshape(1, -1)
    q_a, k_a, v_a = _qkv_proj(h, w["a"], gain_a, None, 0, tm, A_HEADS, A_KV_HEADS, False)
    q_b, k_b, v_b, kb16, vb16 = _qkv_proj(h, w["b"], gain_b, tab128, half128, tm, B_HEADS, B_KV_HEADS, True)
    q_i = _qidx_proj(h, w["qi"], tab64, half64, tm)
    logf, k_i, w_i = _misc_proj(h, w["misc"], p["b_fgate"], tab64, half64, tm)
    gate_a = _gate_proj(h, w["gate_a"], tm)
    gate_b = _gate_proj(h, w["gate_b"], tm)
    return dict(q_a=q_a, k_a=k_a, v_a=v_a, logf=logf, q_b=q_b, k_b=k_b, v_b=v_b, kb16=kb16, vb16=vb16,
                q_i=q_i, k_i=k_i, w_i=w_i, gate_a=gate_a, gate_b=gate_b)


def _back(x2, f, o_a, o_b, mods, p, tm, rows_per_group, tm_peer, te):
    ga1, sh2, sc2, ga2 = mods[2], mods[3], mods[4], mods[5]
    merged = _merge(o_a, o_b, f["gate_a"], f["gate_b"], p["w_branch_a"], p["w_branch_b"], tm)
    x1, h2t = _outproj(x2, merged, p["w_out"], ga1, p["g_norm2"], sc2, sh2, tm, rows_per_group)
    s_t, thr, m1, m2, rz = _peer_query(h2t, p["wq_t"], p["subkeys"], tm_peer)
    return _peer_dense(h2t, p["u16"], p["v16"], s_t, thr, m1, m2, rz, x1, ga2, tm_peer, te, rows_per_group)


def _prompt_group(x, ada, w, p):
    bsz, t, d = x.shape
    tm = _tile(t, 512)
    mods = [ada[:, i].reshape(bsz, 1, d) for i in range(N_ADA)]
    x2 = x.reshape(bsz * t, d)
    f = _front(x2, mods, jnp.arange(t), w, p, tm, t)
    fk = _cumsum_lanes(f["logf"].reshape(bsz, t, A_HEADS).transpose(0, 2, 1))
    o_a = _fox_prompt(f["q_a"], f["k_a"], f["v_a"], fk.transpose(0, 2, 1), fk, bsz, t)
    o_b = _dsa_prompt(f["q_b"], f["q_i"], f["w_i"], f["k_i"], f["kb16"], f["vb16"], bsz, t)
    y = _back(x2, f, o_a, o_b, mods, p, tm, t, _tile(bsz * t, 512), 512)
    return y.reshape(bsz, t, d), f


def _new_key_bias(base, n_heads, n_kv):
    bsz, n_tok = base.shape[0], base.shape[1]
    head_ok = (jnp.arange(n_heads)[:, None] // (n_heads // n_kv)) == jnp.arange(n_kv)[None, :]
    full = jnp.where(head_ok[None, None, :, None, :], base[..., None], NEG)
    full = full.reshape(bsz, n_tok * n_heads, n_tok * n_kv)
    return jnp.pad(full, ((0, 0), (0, 0), (0, PAGE_SIZE - n_tok * n_kv)), constant_values=NEG)


def _pad_rows(a, rows):
    return jnp.pad(a, ((0, 0), (0, rows - a.shape[1]), (0, 0)))


def _sample_group(x, ada, w, p, caches, page_table):
    bsz, n_tok, d = x.shape
    n = bsz * n_tok
    ck_a, cv_a, clf, ck_b, cv_b, cki = caches
    n_pages = page_table.shape[1]
    past = n_pages * PAGE_SIZE
    tm = _tile(n, 512)
    mods = [jnp.repeat(ada[:, i], n_tok, axis=0).reshape(n // tm, tm, d) for i in range(N_ADA)]
    x2 = x.reshape(n, d)
    pos = jnp.tile(past + jnp.arange(n_tok), bsz)
    f = _front(x2, mods, pos, w, p, tm, tm)
    rep_attn, rep_small = _rep(n_pages, 16), _rep(n_pages, 32)
    t_idx = jnp.arange(n_tok)
    causal = t_idx[None, :] <= t_idx[:, None]

    def rows128(a, n_kv):
        return _pad_rows(a.reshape(bsz, n_tok * n_kv, HEAD_DIM), PAGE_SIZE)

    lf_new = jnp.pad(f["logf"].reshape(bsz, n_tok, A_HEADS).transpose(0, 2, 1),
                     ((0, 0), (0, 0), (0, PAGE_SIZE - n_tok)))
    neg_fp, fn_t = _fscan(page_table, clf, lf_new, rep_small, A_KV_HEADS)
    fn = fn_t[:, :, :n_tok].transpose(0, 2, 1)
    base = jnp.where(causal[None, :, None, :], -fn.transpose(0, 2, 1)[:, None, :, :], NEG)
    o_a = _paged_attn(page_table, ck_a, cv_a, f["q_a"].reshape(bsz, n_tok * A_HEADS, HEAD_DIM),
                      neg_fp, fn.reshape(bsz, n_tok * A_HEADS, 1),
                      rows128(f["k_a"], A_KV_HEADS), rows128(f["v_a"], A_KV_HEADS),
                      _new_key_bias(base, A_HEADS, A_KV_HEADS), rep_attn, A_KV_HEADS, A_HEADS, True)
    o_a = o_a.reshape(n, A_HEADS * HEAD_DIM).astype(BF16)

    qi = f["q_i"].reshape(IDX_HEADS, bsz, n_tok, IDX_DIM).transpose(1, 2, 0, 3).reshape(bsz, n_tok * IDX_HEADS, IDX_DIM)
    wi = f["w_i"].reshape(bsz, n_tok * IDX_HEADS, 1)
    key_bias, nbm = _sample_index(page_table, cki, qi, wi,
                                  _pad_rows(f["k_i"].reshape(bsz, n_tok, IDX_DIM), PAGE_SIZE),
                                  rep_small, n_tok, B_KV_HEADS)
    base = jnp.broadcast_to(nbm[:, :n_tok, None, :n_tok], (bsz, n_tok, B_HEADS, n_tok))
    o_b = _paged_attn(page_table, ck_b, cv_b, f["q_b"].reshape(bsz, n_tok * B_HEADS, HEAD_DIM),
                      key_bias, jnp.zeros((bsz, n_tok * B_HEADS, 1), F32),
                      rows128(f["k_b"], B_KV_HEADS), rows128(f["v_b"], B_KV_HEADS),
                      _new_key_bias(base, B_HEADS, B_KV_HEADS), rep_attn, B_KV_HEADS, B_HEADS, False)
    o_b = o_b.reshape(n, B_HEADS * HEAD_DIM).astype(BF16)

    y = _back(x2, f, o_a, o_b, mods, p, tm, tm, _tile(n, 512), 512)
    return y.reshape(bsz, n_tok, d), f


def _layer_outputs(f, bsz, t):
    return (f["k_a"].reshape(bsz, t, A_KV_HEADS, HEAD_DIM), f["v_a"].reshape(bsz, t, A_KV_HEADS, HEAD_DIM),
            f["logf"].reshape(bsz, t, A_HEADS),
            f["k_b"].reshape(bsz, t, B_KV_HEADS, HEAD_DIM), f["v_b"].reshape(bsz, t, B_KV_HEADS, HEAD_DIM),
            f["k_i"].reshape(bsz, t, IDX_DIM))


def kernel(x_prompt, x_sample, cache_fox_k, cache_fox_v, cache_fox_logf, cache_dsa_k, cache_dsa_v, cache_idx_k, page_table, c_prompt, c_sample, w_ada, b_ada, g_norm1, g_norm2, w_in, b_fgate, g_qn_a, g_kn_a, g_qn_b, g_kn_b, w_branch_a, w_branch_b, w_out, w_peer_q, peer_subkeys, peer_u, peer_v):
    depth = w_ada.shape[0]
    bsz, t, d = x_prompt.shape
    bs, ts, _ = x_sample.shape
    xp, xs = x_prompt, x_sample
    rows_p, rows_s = [], []
    n_c = bsz + bs
    c_all = jnp.pad(jnp.concatenate([c_prompt, c_sample], axis=0), ((0, (-n_c) % 8), (0, 0)))
    n_pool = cache_fox_k.shape[1]
    pools = (cache_fox_k.reshape(-1, HEAD_DIM), cache_fox_v.reshape(-1, HEAD_DIM),
             cache_fox_logf.transpose(0, 1, 3, 2).reshape(depth * n_pool, A_HEADS, PAGE_SIZE),
             cache_dsa_k.reshape(-1, HEAD_DIM), cache_dsa_v.reshape(-1, HEAD_DIM),
             cache_idx_k.reshape(depth * n_pool, PAGE_SIZE, IDX_DIM))
    for l in range(depth):
        ada = _ada(c_all, w_ada[l], b_ada[l]).reshape(c_all.shape[0], N_ADA, d)
        w = _split_w_in(w_in[l], d)
        p = dict(g_norm1=g_norm1[l], g_norm2=g_norm2[l], b_fgate=b_fgate[l], g_qn_a=g_qn_a[l], g_kn_a=g_kn_a[l],
                 g_qn_b=g_qn_b[l], g_kn_b=g_kn_b[l],
                 w_branch_a=w_branch_a[l].astype(BF16), w_branch_b=w_branch_b[l].astype(BF16),
                 w_out=w_out[l].astype(BF16), wq_t=w_peer_q[l].T.astype(BF16),
                 subkeys=peer_subkeys[l].reshape(PEER_HEADS * 2, PEER_KEYS, -1).astype(BF16),
                 u16=peer_u[l].astype(BF16), v16=peer_v[l].astype(BF16))
        xp, f_p = _prompt_group(xp, ada[:bsz], w, p)
        xs, f_s = _sample_group(xs, ada[bsz:n_c], w, p, pools, page_table + l * n_pool)
        rows_p.append(_layer_outputs(f_p, bsz, t))
        rows_s.append(_layer_outputs(f_s, bs, ts))
    outs_p = [jnp.stack([r[i] for r in rows_p], axis=0) for i in range(6)]
    outs_s = [jnp.stack([r[i] for r in rows_s], axis=0) for i in range(6)]
    return (xp, xs, *outs_p, *outs_s)
```

```python
import functools

import jax
import jax.numpy as jnp
from jax import lax
from jax.experimental import pallas as pl
from jax.experimental.pallas import tpu as pltpu

F32 = jnp.float32
BF16 = jnp.bfloat16
I32 = jnp.int32

HEAD_DIM = 128
A_HEADS = 8
A_KV_HEADS = 4
B_HEADS = 8
B_KV_HEADS = 4
IDX_HEADS = 16
IDX_DIM = 64
ROPE_THETA = 500000.0
ROPE_FRAC = 4
TOPK_MAX = 256
PEER_KEYS = 128
PEER_HEADS = 8
PEER_TOPK = 16
PAGE_SIZE = 128
EPS = 1e-6
ATTN_SCALE = HEAD_DIM ** -0.5
N_ADA = 6
LANES = 128
MXU_COLS = 256
NEG = -1e30
LOWEST = -1.0e38
BISECT_ITERS = 40
VMEM_LIMIT = 56 * 1024 * 1024


def _cparams(*sem):
    return pltpu.CompilerParams(dimension_semantics=sem, vmem_limit_bytes=VMEM_LIMIT)


def _tile(n, pref, mult=LANES):
    best = None
    t = mult
    while t <= min(n, pref):
        if n % t == 0:
            best = t
        t += mult
    return best if best is not None else n


def _dot(a, b):
    return jnp.dot(a, b, preferred_element_type=F32)


def _dot_nt(a, b):
    return lax.dot_general(a, b, (((1,), (1,)), ((), ())), preferred_element_type=F32)


def _sigmoid(x):
    return 1.0 / (1.0 + jnp.exp(-x))


def _rep(n, pref):
    return max(r for r in range(1, min(n, pref) + 1) if n % r == 0)


def _ada_kernel(c_ref, w_ref, b_ref, o_ref):
    c = c_ref[...]
    s = c * _sigmoid(c)
    o_ref[...] = _dot(s.astype(BF16), w_ref[...].astype(BF16)) + b_ref[...]


def _ada(c_all, w_ada, b_ada):
    m, d = c_all.shape
    n = w_ada.shape[1]
    tn = _tile(n, 1024)
    return pl.pallas_call(
        _ada_kernel,
        grid=(n // tn,),
        in_specs=[pl.BlockSpec((m, d), lambda j: (0, 0)),
                  pl.BlockSpec((d, tn), lambda j: (0, j)),
                  pl.BlockSpec((1, tn), lambda j: (0, j))],
        out_specs=pl.BlockSpec((m, tn), lambda j: (0, j)),
        out_shape=jax.ShapeDtypeStruct((m, n), F32),
        compiler_params=_cparams("arbitrary"),
        name="ada",
    )(c_all, w_ada, b_ada.reshape(1, n))


def _rms_mod(x, g, sc, sh):
    y = x * lax.rsqrt(jnp.mean(x * x, axis=-1, keepdims=True) + EPS)
    return (y * g) * (1.0 + sc) + sh


def _prenorm_kernel(x_ref, g_ref, sc_ref, sh_ref, o_ref):
    o_ref[...] = _rms_mod(x_ref[...], g_ref[...], sc_ref[...], sh_ref[...]).astype(o_ref.dtype)


def _mod_spec(mod, tm, rows_per_group):
    r, d = mod.shape[1], mod.shape[2]
    return pl.BlockSpec((None, r, d), lambda i: ((i * tm) // rows_per_group, 0, 0))


def _prenorm(x, g, sc, sh, tm, rows_per_group):
    n, d = x.shape
    return pl.pallas_call(
        _prenorm_kernel,
        grid=(n // tm,),
        in_specs=[pl.BlockSpec((tm, d), lambda i: (i, 0)),
                  pl.BlockSpec((1, d), lambda i: (0, 0)),
                  _mod_spec(sc, tm, rows_per_group),
                  _mod_spec(sh, tm, rows_per_group)],
        out_specs=pl.BlockSpec((tm, d), lambda i: (i, 0)),
        out_shape=jax.ShapeDtypeStruct((n, d), BF16),
        compiler_params=_cparams("arbitrary"),
        name="prenorm",
    )(x, g.reshape(1, d), sc, sh)


def _rope_tables(pos, head_dim):
    rd = head_dim // ROPE_FRAC
    half = rd // 2
    inv = ROPE_THETA ** (-jnp.arange(half, dtype=F32) / half)
    ang = pos.astype(F32)[:, None] * inv[None, :]
    cos, sin = jnp.cos(ang), jnp.sin(ang)
    t = pos.shape[0]
    rest = head_dim - rd
    c = jnp.concatenate([cos, cos, jnp.ones((t, rest), F32)], axis=-1)
    sa = jnp.concatenate([-sin, jnp.zeros((t, half + rest), F32)], axis=-1)
    sb = jnp.concatenate([jnp.zeros((t, half), F32), sin, jnp.zeros((t, rest), F32)], axis=-1)
    reps = LANES // head_dim
    return tuple(jnp.tile(a, (1, reps)) for a in (c, sa, sb)), half


def _rope(y, c, sa, sb, half):
    return y * c + pltpu.roll(y, LANES - half, 1) * sa + pltpu.roll(y, half, 1) * sb


def _head_norm(chunk, gain):
    y = chunk * lax.rsqrt(jnp.mean(chunk * chunk, axis=-1, keepdims=True) + EPS)
    return y * gain


def _qkv_kernel(*refs, n_q, n_k, rope_half):
    h_ref, w_ref, gain_ref = refs[:3]
    pos = 3
    if rope_half:
        c_ref, sa_ref, sb_ref = refs[3:6]
        pos = 6
    q_ref, k_ref, v_ref, kb_ref, vb_ref = refs[pos:pos + 5]
    tm = h_ref.shape[0]
    z = _dot(h_ref[...], w_ref[...])
    for ch in range(n_q + n_k):
        sl = slice(ch * LANES, (ch + 1) * LANES)
        y = _head_norm(z[:, sl], gain_ref[:, sl])
        if rope_half:
            y = _rope(y, c_ref[...], sa_ref[...], sb_ref[...], rope_half)
        if ch < n_q:
            q_ref[:, sl] = (y * ATTN_SCALE).astype(q_ref.dtype)
        else:
            j = ch - n_q
            k_ref[pl.ds(j, tm, stride=n_k), :] = y
            kb_ref[:, j * LANES:(j + 1) * LANES] = y.astype(BF16)
    for j in range(n_k):
        v = z[:, (n_q + n_k + j) * LANES:(n_q + n_k + j + 1) * LANES]
        v_ref[pl.ds(j, tm, stride=n_k), :] = v
        vb_ref[:, j * LANES:(j + 1) * LANES] = v.astype(BF16)


def _qkv_proj(h, w, gain, tables, rope_half, tm, n_q, n_k):
    n, d = h.shape
    wq, wk = n_q * LANES, n_k * LANES
    width = w.shape[1]
    row = lambda i: (i, 0)
    const = lambda i: (0, 0)
    in_specs = [pl.BlockSpec((tm, d), row), pl.BlockSpec((d, width), const),
                pl.BlockSpec((1, wq + wk), const)]
    args = [h, w, gain]
    if rope_half:
        nt = tables[0].shape[0] // tm
        for t in tables:
            in_specs.append(pl.BlockSpec((tm, LANES), lambda i: (i % nt, 0)))
            args.append(t)
    out_shape = ([jax.ShapeDtypeStruct((n, wq), BF16)] + [jax.ShapeDtypeStruct((n * n_k, LANES), F32)] * 2
                 + [jax.ShapeDtypeStruct((n, wk), BF16)] * 2)
    out_specs = ([pl.BlockSpec((tm, wq), row)] + [pl.BlockSpec((tm * n_k, LANES), row)] * 2
                 + [pl.BlockSpec((tm, wk), row)] * 2)
    return pl.pallas_call(
        functools.partial(_qkv_kernel, n_q=n_q, n_k=n_k, rope_half=rope_half),
        grid=(n // tm,), in_specs=in_specs, out_specs=out_specs, out_shape=out_shape,
        compiler_params=_cparams("arbitrary"), name="qkv_proj",
    )(*args)


def _qidx_kernel(h_ref, w_ref, c_ref, sa_ref, sb_ref, o_ref, *, half):
    z = _dot(h_ref[...], w_ref[...])
    for ch in range(IDX_HEADS // 2):
        y = _rope(z[:, ch * LANES:(ch + 1) * LANES], c_ref[...], sa_ref[...], sb_ref[...], half)
        o_ref[2 * ch] = y[:, :IDX_DIM].astype(o_ref.dtype)
        o_ref[2 * ch + 1] = y[:, IDX_DIM:].astype(o_ref.dtype)


def _qidx_proj(h, w, tables, half, tm):
    n, d = h.shape
    nt = tables[0].shape[0] // tm
    tab = pl.BlockSpec((tm, LANES), lambda i: (i % nt, 0))
    return pl.pallas_call(
        functools.partial(_qidx_kernel, half=half),
        grid=(n // tm,),
        in_specs=[pl.BlockSpec((tm, d), lambda i: (i, 0)),
                  pl.BlockSpec((d, IDX_HEADS * IDX_DIM), lambda i: (0, 0)), tab, tab, tab],
        out_specs=pl.BlockSpec((IDX_HEADS, tm, IDX_DIM), lambda i: (0, i, 0)),
        out_shape=jax.ShapeDtypeStruct((IDX_HEADS, n, IDX_DIM), BF16),
        compiler_params=_cparams("arbitrary"), name="qidx_proj",
    )(h, w, *tables)


def _misc_kernel(h_ref, w_ref, bf_ref, c_ref, sa_ref, sb_ref, logf_ref, ki_ref, wi_ref, *, half):
    z = _dot(h_ref[...], w_ref[...])
    f = z[:, :A_HEADS] + bf_ref[...]
    logf_ref[...] = jnp.minimum(f, 0.0) - jnp.log1p(jnp.exp(-jnp.abs(f)))
    y = _rope(z[:, LANES:2 * LANES], c_ref[...], sa_ref[...], sb_ref[...], half)
    ki_ref[...] = y[:, :IDX_DIM]
    wi_ref[...] = z[:, 2 * LANES:2 * LANES + IDX_HEADS]


def _misc_proj(h, w, b_fgate, tables, half, tm):
    n, d = h.shape
    nt = tables[0].shape[0] // tm
    tab = pl.BlockSpec((tm, LANES), lambda i: (i % nt, 0))
    row = lambda i: (i, 0)
    return pl.pallas_call(
        functools.partial(_misc_kernel, half=half),
        grid=(n // tm,),
        in_specs=[pl.BlockSpec((tm, d), row), pl.BlockSpec((d, 3 * LANES), lambda i: (0, 0)),
                  pl.BlockSpec((1, A_HEADS), lambda i: (0, 0)), tab, tab, tab],
        out_specs=[pl.BlockSpec((tm, A_HEADS), row), pl.BlockSpec((tm, IDX_DIM), row),
                   pl.BlockSpec((tm, IDX_HEADS), row)],
        out_shape=[jax.ShapeDtypeStruct((n, A_HEADS), F32), jax.ShapeDtypeStruct((n, IDX_DIM), F32),
                   jax.ShapeDtypeStruct((n, IDX_HEADS), F32)],
        compiler_params=_cparams("arbitrary"), name="misc_proj",
    )(h, w, b_fgate.reshape(1, A_HEADS), *tables)


def _gate_kernel(h_ref, w_ref, o_ref):
    o_ref[...] = _sigmoid(_dot(h_ref[...], w_ref[...]))


def _gate_proj(h, w, tm):
    n, d = h.shape
    width = w.shape[1]
    return pl.pallas_call(
        _gate_kernel, grid=(n // tm,),
        in_specs=[pl.BlockSpec((tm, d), lambda i: (i, 0)), pl.BlockSpec((d, width), lambda i: (0, 0))],
        out_specs=pl.BlockSpec((tm, width), lambda i: (i, 0)),
        out_shape=jax.ShapeDtypeStruct((n, width), F32),
        compiler_params=_cparams("arbitrary"), name="gate_proj",
    )(h, w)


def _cumsum_kernel(x_ref, o_ref, *, tc):
    t = x_ref.shape[-1]
    r = lax.broadcasted_iota(I32, (tc, tc), 0)
    c = lax.broadcasted_iota(I32, (tc, tc), 1)
    tri = jnp.where(r <= c, 1.0, 0.0).astype(F32)
    carry = jnp.zeros((x_ref.shape[0], 1), F32)
    for blk in range(t // tc):
        sl = slice(blk * tc, (blk + 1) * tc)
        cs = jnp.dot(x_ref[:, sl], tri, precision=lax.Precision.HIGHEST, preferred_element_type=F32) + carry
        o_ref[:, sl] = cs
        carry = cs[:, tc - 1:tc]


def _cumsum_lanes(x):
    b, h, t = x.shape
    tc = _tile(t, 512)
    return pl.pallas_call(
        functools.partial(_cumsum_kernel, tc=tc), grid=(b,),
        in_specs=[pl.BlockSpec((None, h, t), lambda i: (i, 0, 0))],
        out_specs=pl.BlockSpec((None, h, t), lambda i: (i, 0, 0)),
        out_shape=jax.ShapeDtypeStruct((b, h, t), F32),
        compiler_params=_cparams("arbitrary"), name="cumsum",
    )(x)


def _online_update(s, v16, m_ref, l_ref, acc_ref):
    m_prev = m_ref[...]
    m_new = jnp.maximum(m_prev, jnp.max(s, axis=-1, keepdims=True))
    alpha = jnp.exp(m_prev - m_new)
    p = jnp.exp(s - m_new)
    l_ref[...] = alpha * l_ref[...] + jnp.sum(p, axis=-1, keepdims=True)
    acc_ref[...] = alpha * acc_ref[...] + _dot(p.astype(BF16), v16)
    m_ref[...] = m_new


def _fox_kernel(q_ref, k_ref, v_ref, fq_ref, fk_ref, o_ref, m_sc, l_sc, acc_sc, *, tq, tk, group):
    i = pl.program_id(1)
    kk = pl.program_id(2)
    n_heads = q_ref.shape[1] // LANES

    @pl.when(kk == 0)
    def _():
        m_sc[...] = jnp.full(m_sc.shape, NEG, F32)
        l_sc[...] = jnp.zeros(l_sc.shape, F32)
        acc_sc[...] = jnp.zeros(acc_sc.shape, F32)

    @pl.when(kk * tk <= i * tq + tq - 1)
    def _():
        row = i * tq + lax.broadcasted_iota(I32, (tq, tk), 0)
        col = kk * tk + lax.broadcasted_iota(I32, (tq, tk), 1)
        causal = col <= row
        for h in range(n_heads):
            kv = slice((h // group) * LANES, (h // group + 1) * LANES)
            s = _dot_nt(q_ref[:, h * LANES:(h + 1) * LANES], k_ref[:, kv])
            s = s + fq_ref[:, h:h + 1] - fk_ref[h:h + 1, :]
            s = jnp.where(causal, s, NEG)
            _online_update(s, v_ref[:, kv], m_sc.at[h], l_sc.at[h], acc_sc.at[h])

    @pl.when(kk == pl.num_programs(2) - 1)
    def _():
        for h in range(n_heads):
            o_ref[:, h * LANES:(h + 1) * LANES] = (acc_sc[h] / l_sc[h]).astype(o_ref.dtype)


def _fox_prompt(q, k, v, fq, fk, bsz, t):
    tq = _tile(t, 256)
    tk = _tile(t, 1024)
    nq, nk = t // tq, t // tk
    kvw = A_KV_HEADS * LANES
    last_k = lambda i, kk: jnp.minimum(kk, (i * tq + tq - 1) // tk)
    return pl.pallas_call(
        functools.partial(_fox_kernel, tq=tq, tk=tk, group=A_HEADS // A_KV_HEADS),
        grid=(bsz, nq, nk),
        in_specs=[pl.BlockSpec((tq, A_HEADS * LANES), lambda b, i, kk: (b * nq + i, 0)),
                  pl.BlockSpec((tk, kvw), lambda b, i, kk: (b * nk + last_k(i, kk), 0)),
                  pl.BlockSpec((tk, kvw), lambda b, i, kk: (b * nk + last_k(i, kk), 0)),
                  pl.BlockSpec((None, tq, A_HEADS), lambda b, i, kk: (b, i, 0)),
                  pl.BlockSpec((None, A_HEADS, tk), lambda b, i, kk: (b, 0, last_k(i, kk)))],
        out_specs=pl.BlockSpec((tq, A_HEADS * LANES), lambda b, i, kk: (b * nq + i, 0)),
        out_shape=jax.ShapeDtypeStruct((bsz * t, A_HEADS * LANES), BF16),
        scratch_shapes=[pltpu.VMEM((A_HEADS, tq, 1), F32), pltpu.VMEM((A_HEADS, tq, 1), F32),
                        pltpu.VMEM((A_HEADS, tq, LANES), F32)],
        compiler_params=_cparams("arbitrary", "arbitrary", "arbitrary"),
        name="fox_prompt",
    )(q, k, v, fq, fk)


def _kth_value(count_ge, lo, hi, c_lo, k):
    def cond(st):
        return jnp.logical_and(st[0] < BISECT_ITERS, jnp.max(st[3]) > k)

    def body(st):
        it, lo, hi, c_lo = st
        mid = 0.5 * lo + 0.5 * hi
        c = count_ge(mid)
        ge = c >= k
        return it + 1, jnp.where(ge, mid, lo), jnp.where(ge, hi, mid), jnp.where(ge, c, c_lo)

    return lax.while_loop(cond, body, (jnp.int32(0), lo, hi, c_lo))[1]


def _dsa_kernel(qb_ref, qi_ref, wi_ref, ki_ref, kb_ref, vb_ref, o_ref, key_sc, m_sc, l_sc, acc_sc,
                *, tq, tk, topk):
    i = pl.program_id(1)
    nch = (i * tq + tq + tk - 1) // tk
    row = i * tq + lax.broadcasted_iota(I32, (tq, 1), 0)

    def scores(c, carry):
        mx, mn = carry
        kc = ki_ref[pl.ds(pl.multiple_of(c * tk, tk), tk), :].astype(BF16)
        acc = jnp.zeros((tq, tk), F32)
        for h in range(IDX_HEADS):
            acc = acc + wi_ref[:, h:h + 1] * jnp.maximum(_dot_nt(qi_ref[h], kc), 0.0)
        valid = (c * tk + lax.broadcasted_iota(I32, (1, tk), 1)) <= row
        key_sc[c] = jnp.where(valid, acc, -jnp.inf)
        mx = jnp.maximum(mx, jnp.max(jnp.where(valid, acc, -jnp.inf), axis=-1, keepdims=True))
        mn = jnp.minimum(mn, jnp.min(jnp.where(valid, acc, jnp.inf), axis=-1, keepdims=True))
        return mx, mn

    mx, mn = lax.fori_loop(0, nch, scores, (jnp.full((tq, 1), -jnp.inf, F32), jnp.full((tq, 1), jnp.inf, F32)))

    def count_ge(cand):
        def cbody(c, cnt):
            ge = jnp.where(key_sc[c] >= cand, 1.0, 0.0)
            for s in range(tk // LANES):
                cnt = cnt + ge[:, s * LANES:(s + 1) * LANES]
            return cnt
        cnt = lax.fori_loop(0, nch, cbody, jnp.zeros((tq, LANES), F32))
        return jnp.sum(cnt, axis=-1, keepdims=True)

    many = row >= topk
    lo0 = jnp.where(many, mn, LOWEST)
    hi0 = jnp.where(many, mx, LOWEST)
    c0 = jnp.where(many, (row + 1).astype(F32), 0.0)
    thr = _kth_value(count_ge, lo0, hi0, c0, float(topk))

    m_sc[...] = jnp.full(m_sc.shape, NEG, F32)
    l_sc[...] = jnp.zeros(l_sc.shape, F32)
    acc_sc[...] = jnp.zeros(acc_sc.shape, F32)

    def attend(c, carry):
        bias = jnp.where(key_sc[c] >= thr, 0.0, NEG)
        rows = pl.ds(pl.multiple_of(c * tk, tk), tk)
        for h in range(B_HEADS):
            j = h // (B_HEADS // B_KV_HEADS)
            k16 = kb_ref[rows, j * LANES:(j + 1) * LANES]
            v16 = vb_ref[rows, j * LANES:(j + 1) * LANES]
            s = _dot_nt(qb_ref[:, h * LANES:(h + 1) * LANES], k16) + bias
            _online_update(s, v16, m_sc.at[h], l_sc.at[h], acc_sc.at[h])
        return carry

    lax.fori_loop(0, nch, attend, 0)
    for h in range(B_HEADS):
        o_ref[:, h * LANES:(h + 1) * LANES] = (acc_sc[h] / l_sc[h]).astype(o_ref.dtype)


def _dsa_prompt(qb, qi, wi, ki, kb16, vb16, bsz, t):
    tq = _tile(t, 256)
    tk = _tile(t, 1024)
    nq = t // tq
    topk = min(TOPK_MAX, t // 4)
    kvw = B_KV_HEADS * LANES
    return pl.pallas_call(
        functools.partial(_dsa_kernel, tq=tq, tk=tk, topk=topk),
        grid=(bsz, nq),
        in_specs=[pl.BlockSpec((tq, B_HEADS * LANES), lambda b, i: (b * nq + i, 0)),
                  pl.BlockSpec((IDX_HEADS, tq, IDX_DIM), lambda b, i: (0, b * nq + i, 0)),
                  pl.BlockSpec((tq, IDX_HEADS), lambda b, i: (b * nq + i, 0)),
                  pl.BlockSpec((t, IDX_DIM), lambda b, i: (b, 0)),
                  pl.BlockSpec((t, kvw), lambda b, i: (b, 0)),
                  pl.BlockSpec((t, kvw), lambda b, i: (b, 0))],
        out_specs=pl.BlockSpec((tq, B_HEADS * LANES), lambda b, i: (b * nq + i, 0)),
        out_shape=jax.ShapeDtypeStruct((bsz * t, B_HEADS * LANES), BF16),
        scratch_shapes=[pltpu.VMEM((t // tk, tq, tk), F32),
                        pltpu.VMEM((B_HEADS, tq, 1), F32), pltpu.VMEM((B_HEADS, tq, 1), F32),
                        pltpu.VMEM((B_HEADS, tq, LANES), F32)],
        compiler_params=_cparams("arbitrary", "arbitrary"),
        name="dsa_prompt",
    )(qb, qi, wi, ki, kb16, vb16)


def _page_specs(block, n_rep):
    def spec(r):
        return pl.BlockSpec(block, lambda b, pg, pt: (pt[b, pg * n_rep + r],) + (0,) * (len(block) - 1))
    return [spec(r) for r in range(n_rep)]


def _spread_matrix(n_kv, lower_tri):
    r = lax.broadcasted_iota(I32, (PAGE_SIZE, PAGE_SIZE * n_kv), 0)
    key = lax.shift_right_logical(lax.broadcasted_iota(I32, (PAGE_SIZE, PAGE_SIZE * n_kv), 1),
                                  n_kv.bit_length() - 1)
    return jnp.where((r <= key) if lower_tri else (r == key), 1.0, 0.0)


def _fscan_kernel(pt_ref, *refs, n_rep, n_kv):
    pages = refs[:n_rep]
    new_ref, fp_ref, fn_ref, carry_sc = refs[n_rep:]
    pg = pl.program_id(1)
    h = new_ref.shape[0]
    wide = PAGE_SIZE * n_kv
    m = n_rep * h
    hp = lax.Precision.HIGHEST

    @pl.when(pg == 0)
    def _():
        carry_sc[...] = jnp.zeros(carry_sc.shape, F32)

    x = jnp.concatenate([pages[rr][...] for rr in range(n_rep)], axis=0)
    cs = jnp.dot(x, _spread_matrix(n_kv, True), precision=hp, preferred_element_type=F32)
    tot = cs[:, wide - 1:wide]
    i = lax.broadcasted_iota(I32, (m, m), 0)
    j = lax.broadcasted_iota(I32, (m, m), 1)
    sh = h.bit_length() - 1
    earlier_page = jnp.where(lax.shift_right_logical(j, sh) < lax.shift_right_logical(i, sh), 1.0, 0.0)
    lower = jnp.where((i & (h - 1)) == (j & (h - 1)), earlier_page, 0.0)
    pref = jnp.dot(lower, jnp.broadcast_to(tot, (m, LANES)), precision=hp, preferred_element_type=F32)[:, :1]
    carry = carry_sc[...]
    out = cs + (pref + jnp.concatenate([carry] * n_rep, axis=0))
    for rr in range(n_rep):
        fp_ref[:, rr * wide:(rr + 1) * wide] = -out[rr * h:(rr + 1) * h, :]
    carry = carry + pref[m - h:, :] + tot[m - h:, :]
    carry_sc[...] = carry

    @pl.when(pg == pl.num_programs(1) - 1)
    def _():
        r = lax.broadcasted_iota(I32, (PAGE_SIZE, PAGE_SIZE), 0)
        c = lax.broadcasted_iota(I32, (PAGE_SIZE, PAGE_SIZE), 1)
        tri = jnp.where(r <= c, 1.0, 0.0)
        fn_ref[...] = jnp.dot(new_ref[...], tri, precision=hp, preferred_element_type=F32) + carry


def _fscan(page_table, logf_pool_t, logf_new_t, n_rep, n_kv):
    bsz, n_pages = page_table.shape
    h = logf_pool_t.shape[1]
    wide = PAGE_SIZE * n_kv
    grid_spec = pltpu.PrefetchScalarGridSpec(
        num_scalar_prefetch=1, grid=(bsz, n_pages // n_rep),
        in_specs=_page_specs((None, h, PAGE_SIZE), n_rep)
        + [pl.BlockSpec((None, h, PAGE_SIZE), lambda b, pg, pt: (b, 0, 0))],
        out_specs=[pl.BlockSpec((None, h, n_rep * wide), lambda b, pg, pt: (b, 0, pg)),
                   pl.BlockSpec((None, h, PAGE_SIZE), lambda b, pg, pt: (b, 0, 0))],
        scratch_shapes=[pltpu.VMEM((h, 1), F32)])
    return pl.pallas_call(
        functools.partial(_fscan_kernel, n_rep=n_rep, n_kv=n_kv), grid_spec=grid_spec,
        out_shape=[jax.ShapeDtypeStruct((bsz, h, n_pages * wide), F32),
                   jax.ShapeDtypeStruct((bsz, h, PAGE_SIZE), F32)],
        compiler_params=_cparams("arbitrary", "arbitrary"), name="fscan",
    )(page_table, *([logf_pool_t] * n_rep), logf_new_t)


def _sidx_kernel(pt_ref, *refs, n_rep, n_tok, topk, n_kv):
    pages = refs[:n_rep]
    qi_ref, wi_ref, kn_ref, kb_ref, nb_ref, sc_sc = refs[n_rep:]
    pg = pl.program_id(1)
    npg = pl.num_programs(1)
    rows = 8
    pad = jnp.zeros((rows - n_tok, PAGE_SIZE), F32)

    def score(keys_f32):
        d = jnp.maximum(_dot_nt(qi_ref[...], keys_f32.astype(BF16)), 0.0) * wi_ref[...]
        return jnp.concatenate([jnp.sum(d.reshape(n_tok, IDX_HEADS, PAGE_SIZE), axis=1), pad], axis=0)

    sc_sc[pg] = jnp.concatenate([score(pages[rr][...]) for rr in range(n_rep)], axis=1)

    @pl.when(pg == npg - 1)
    def _():
        t_idx = lax.broadcasted_iota(I32, (rows, PAGE_SIZE), 0)
        s_idx = lax.broadcasted_iota(I32, (rows, PAGE_SIZE), 1)
        new_valid = (s_idx <= t_idx) & (t_idx < n_tok)
        new_raw = score(kn_ref[...])
        new = jnp.where(new_valid, new_raw, -jnp.inf)
        past = sc_sc[...]

        def count_ge(cand):
            c_past = jnp.sum(jnp.sum(jnp.where(past >= cand[None], 1.0, 0.0), axis=0), axis=-1, keepdims=True)
            return c_past + jnp.sum(jnp.where(new >= cand, 1.0, 0.0), axis=-1, keepdims=True)

        mx = jnp.maximum(jnp.max(jnp.max(past, axis=0), axis=-1, keepdims=True),
                         jnp.max(new, axis=-1, keepdims=True))
        mn = jnp.minimum(jnp.min(jnp.min(past, axis=0), axis=-1, keepdims=True),
                         jnp.min(jnp.where(new_valid, new_raw, jnp.inf), axis=-1, keepdims=True))
        tok = lax.broadcasted_iota(I32, (rows, 1), 0)
        n_valid = jnp.where(tok < n_tok, (past.shape[0] * past.shape[2] + 1 + tok).astype(F32), 0.0)
        many = n_valid > topk
        thr = _kth_value(count_ge, jnp.where(many, mn, LOWEST), jnp.where(many, mx, LOWEST),
                         jnp.where(many, n_valid, 0.0), float(topk))
        spread = _spread_matrix(n_kv, False).astype(BF16)
        wide = PAGE_SIZE * n_kv
        for s in range(past.shape[0]):
            sel = jnp.where(past[s] >= thr, 1.0, 0.0)
            stacked = jnp.concatenate([sel[:, c * PAGE_SIZE:(c + 1) * PAGE_SIZE] for c in range(n_rep)], axis=0)
            ex = _dot(stacked.astype(BF16), spread)
            for c in range(n_rep):
                kb_ref[:, (s * n_rep + c) * wide:(s * n_rep + c + 1) * wide] = jnp.where(
                    ex[c * rows:(c + 1) * rows, :] > 0.5, 0.0, NEG)
        nb_ref[...] = jnp.where(new >= thr, 0.0, NEG)


def _sample_index(page_table, idx_pool, qi, wi, ki_new, n_rep, n_tok, n_kv):
    bsz, n_pages = page_table.shape
    p = n_pages * PAGE_SIZE
    topk = min(TOPK_MAX, (p + n_tok) // 4)
    nr = n_tok * IDX_HEADS
    steps = n_pages // n_rep
    wide = n_rep * PAGE_SIZE
    grid_spec = pltpu.PrefetchScalarGridSpec(
        num_scalar_prefetch=1, grid=(bsz, steps),
        in_specs=_page_specs((None, PAGE_SIZE, IDX_DIM), n_rep)
        + [pl.BlockSpec((None, nr, IDX_DIM), lambda b, pg, pt: (b, 0, 0)),
           pl.BlockSpec((None, nr, 1), lambda b, pg, pt: (b, 0, 0)),
           pl.BlockSpec((None, PAGE_SIZE, IDX_DIM), lambda b, pg, pt: (b, 0, 0))],
        out_specs=[pl.BlockSpec((None, 8, p * n_kv), lambda b, pg, pt: (b, 0, 0)),
                   pl.BlockSpec((None, 8, PAGE_SIZE), lambda b, pg, pt: (b, 0, 0))],
        scratch_shapes=[pltpu.VMEM((steps, 8, wide), F32)])
    return pl.pallas_call(
        functools.partial(_sidx_kernel, n_rep=n_rep, n_tok=n_tok, topk=topk, n_kv=n_kv), grid_spec=grid_spec,
        out_shape=[jax.ShapeDtypeStruct((bsz, 8, p * n_kv), F32),
                   jax.ShapeDtypeStruct((bsz, 8, PAGE_SIZE), F32)],
        compiler_params=_cparams("arbitrary", "arbitrary"), name="sample_index",
    )(page_table, *([idx_pool] * n_rep), qi, wi, ki_new)


def _paged_attn_kernel(pt_ref, *refs, n_rep, n_kv, n_heads, per_head):
    kp = refs[:n_rep]
    vp = refs[n_rep:2 * n_rep]
    q_ref, kb_ref, rb_ref, kn_ref, vn_ref, nb_ref, o_ref, m_sc, l_sc, acc_sc = refs[2 * n_rep:]
    pg = pl.program_id(1)
    nrow = q_ref.shape[0]
    wide = PAGE_SIZE * n_kv

    @pl.when(pg == 0)
    def _():
        m_sc[...] = jnp.full(m_sc.shape, NEG, F32)
        l_sc[...] = jnp.zeros(l_sc.shape, F32)
        acc_sc[...] = jnp.zeros(acc_sc.shape, F32)

    def expand(kb):
        if per_head:
            return jnp.concatenate([kb] * (nrow // kb.shape[0]), axis=0)
        return jnp.concatenate([jnp.broadcast_to(kb[t:t + 1], (8, kb.shape[1])) for t in range(nrow // 8)], axis=0)

    head = lax.broadcasted_iota(I32, (nrow, wide), 0) % n_heads
    kv_col = lax.broadcasted_iota(I32, (nrow, wide), 1) % n_kv
    head_bias = jnp.where(head // (n_heads // n_kv) == kv_col, 0.0, NEG)
    q = q_ref[...]
    rb = rb_ref[...]

    def update(s, vs, width):
        m_prev = m_sc[...]
        m_new = jnp.maximum(m_prev, jnp.max(s, axis=-1, keepdims=True))
        alpha = jnp.exp(m_prev - m_new)
        p = jnp.exp(s - m_new)
        l_sc[...] = alpha * l_sc[...] + jnp.sum(p, axis=-1, keepdims=True)
        acc = alpha * acc_sc[...]
        for r, v in enumerate(vs):
            acc = acc + _dot(p[:, r * width:(r + 1) * width].astype(BF16), v.astype(BF16))
        acc_sc[...] = acc
        m_sc[...] = m_new

    s = jnp.concatenate([_dot_nt(q, kp[r][...].astype(BF16)) for r in range(n_rep)], axis=1)
    bias = expand(kb_ref[...]) + jnp.concatenate([head_bias] * n_rep, axis=1)
    update(s + rb + bias, [vp[r][...] for r in range(n_rep)], wide)

    @pl.when(pg == pl.num_programs(1) - 1)
    def _():
        sn = _dot_nt(q, kn_ref[...].astype(BF16)) + rb + nb_ref[...]
        update(sn, [vn_ref[...]], PAGE_SIZE)
        o_ref[...] = acc_sc[...] / l_sc[...]


def _paged_attn(page_table, k_pool, v_pool, q, key_bias, row_bias, k_new, v_new, new_bias, n_rep, n_kv, n_heads,
                per_head):
    bsz, n_pages = page_table.shape
    nrow, w = q.shape[1], q.shape[2]
    kbh = key_bias.shape[1]
    wide = PAGE_SIZE * n_kv
    const = lambda b, pg, pt: (b, 0, 0)
    grid_spec = pltpu.PrefetchScalarGridSpec(
        num_scalar_prefetch=1, grid=(bsz, n_pages // n_rep),
        in_specs=_page_specs((wide, w), n_rep) + _page_specs((wide, w), n_rep)
        + [pl.BlockSpec((None, nrow, w), const),
           pl.BlockSpec((None, kbh, n_rep * wide), lambda b, pg, pt: (b, 0, pg)),
           pl.BlockSpec((None, nrow, 1), const),
           pl.BlockSpec((None, PAGE_SIZE, w), const),
           pl.BlockSpec((None, PAGE_SIZE, w), const),
           pl.BlockSpec((None, nrow, PAGE_SIZE), const)],
        out_specs=pl.BlockSpec((None, nrow, w), const),
        scratch_shapes=[pltpu.VMEM((nrow, 1), F32), pltpu.VMEM((nrow, 1), F32), pltpu.VMEM((nrow, w), F32)])
    return pl.pallas_call(
        functools.partial(_paged_attn_kernel, n_rep=n_rep, n_kv=n_kv, n_heads=n_heads, per_head=per_head),
        grid_spec=grid_spec,
        out_shape=jax.ShapeDtypeStruct((bsz, nrow, w), F32),
        compiler_params=_cparams("arbitrary", "arbitrary"), name="paged_attn",
    )(page_table, *([k_pool] * n_rep), *([v_pool] * n_rep), q, key_bias, row_bias, k_new, v_new, new_bias)


def _merge_kernel(oa_ref, ob_ref, ga_ref, gb_ref, wa_ref, wb_ref, o_ref):
    m = ga_ref[...] * _dot(oa_ref[...], wa_ref[...]) + gb_ref[...] * _dot(ob_ref[...], wb_ref[...])
    o_ref[...] = m.astype(o_ref.dtype)


def _merge(oa, ob, ga, gb, wa, wb, tm):
    n, e = oa.shape
    d = wa.shape[1]
    row = lambda i: (i, 0)
    const = lambda i: (0, 0)
    return pl.pallas_call(
        _merge_kernel, grid=(n // tm,),
        in_specs=[pl.BlockSpec((tm, e), row), pl.BlockSpec((tm, e), row),
                  pl.BlockSpec((tm, d), row), pl.BlockSpec((tm, d), row),
                  pl.BlockSpec((e, d), const), pl.BlockSpec((e, d), const)],
        out_specs=pl.BlockSpec((tm, d), row),
        out_shape=jax.ShapeDtypeStruct((n, d), BF16),
        compiler_params=_cparams("arbitrary"), name="merge",
    )(oa, ob, ga, gb, wa, wb)


def _outproj_kernel(x_ref, m_ref, w_ref, g1_ref, gn_ref, sc_ref, sh_ref, x1_ref, h2t_ref):
    x1 = x_ref[...] + g1_ref[...] * _dot(m_ref[...], w_ref[...])
    x1_ref[...] = x1
    h2t_ref[...] = _rms_mod(x1, gn_ref[...], sc_ref[...], sh_ref[...]).T.astype(h2t_ref.dtype)


def _outproj(x, merged, w_out, ga1, g_norm2, sc2, sh2, tm, rows_per_group):
    n, d = x.shape
    row = lambda i: (i, 0)
    return pl.pallas_call(
        _outproj_kernel, grid=(n // tm,),
        in_specs=[pl.BlockSpec((tm, d), row), pl.BlockSpec((tm, d), row),
                  pl.BlockSpec((d, d), lambda i: (0, 0)),
                  _mod_spec(ga1, tm, rows_per_group),
                  pl.BlockSpec((1, d), lambda i: (0, 0)),
                  _mod_spec(sc2, tm, rows_per_group), _mod_spec(sh2, tm, rows_per_group)],
        out_specs=[pl.BlockSpec((tm, d), row), pl.BlockSpec((d, tm), lambda i: (0, i))],
        out_shape=[jax.ShapeDtypeStruct((n, d), F32), jax.ShapeDtypeStruct((d, n), BF16)],
        compiler_params=_cparams("arbitrary"), name="outproj",
    )(x, merged, w_out, ga1, g_norm2.reshape(1, d), sc2, sh2)


def _top_values(x, n):
    out = []
    for _ in range(n):
        m = jnp.max(x, axis=0, keepdims=True)
        out.append(m)
        x = jnp.where(x == m, -jnp.inf, x)
    return out


def _peer_query_kernel(ht_ref, wq_ref, sk_ref, s_ref, thr_ref, m1_ref, m2_ref, rz_ref, sv_sc):
    qt = _dot(wq_ref[...], ht_ref[...])
    n_hc = sk_ref.shape[0]
    for hc in range(n_hc):
        s = _dot(sk_ref[hc], qt[hc * PEER_KEYS:(hc + 1) * PEER_KEYS, :].astype(BF16))
        s_ref[hc] = s
        for r, m in enumerate(_top_values(s, PEER_TOPK)):
            sv_sc[hc, r:r + 1, :] = m
    for h in range(n_hc // 2):
        sv1 = sv_sc[2 * h]
        sv2 = sv_sc[2 * h + 1]
        cands = [sv1[0:1] + sv2]
        cands += [sv1[a:a + 1] + sv2[0:8] for a in range(1, 8)]
        cands += [sv1[8:16] + sv2[0:1]]
        tops = _top_values(jnp.concatenate(cands, axis=0), PEER_TOPK)
        z = jnp.zeros_like(tops[0])
        for m in tops:
            z = z + jnp.exp(m - tops[0])
        thr_ref[h:h + 1, :] = tops[-1]
        m1_ref[h:h + 1, :] = sv1[0:1]
        m2_ref[h:h + 1, :] = sv2[0:1]
        rz_ref[h:h + 1, :] = 1.0 / z


def _peer_query(h2t, wqt, subkeys, tm):
    d, n = h2t.shape
    n_hc = subkeys.shape[0]
    col = lambda i: (0, i)
    small = jax.ShapeDtypeStruct((PEER_HEADS, n), F32)
    small_spec = pl.BlockSpec((PEER_HEADS, tm), col)
    return pl.pallas_call(
        _peer_query_kernel, grid=(n // tm,),
        in_specs=[pl.BlockSpec((d, tm), col), pl.BlockSpec(wqt.shape, lambda i: (0, 0)),
                  pl.BlockSpec(subkeys.shape, lambda i: (0, 0, 0))],
        out_specs=[pl.BlockSpec((n_hc, PEER_KEYS, tm), lambda i: (0, 0, i))] + [small_spec] * 4,
        out_shape=[jax.ShapeDtypeStruct((n_hc, PEER_KEYS, n), F32)] + [small] * 4,
        scratch_shapes=[pltpu.VMEM((n_hc, PEER_TOPK, tm), F32)],
        compiler_params=_cparams("arbitrary"), name="peer_query",
    )(h2t, wqt, subkeys)


def _peer_weighted_acts(tile, live, a_ref, wa_ref, s_ref, thr_ref, m1_ref, e2_sc):
    ts, tm = a_ref.shape
    half = PEER_KEYS // 2

    def block(q, l):
        def run(after):
            i1 = jnp.clip(tile * (ts // PEER_KEYS) + q, 0, PEER_KEYS - 1)
            lanes = slice(l * LANES, (l + 1) * LANES)
            s1 = [s_ref[2 * h, pl.ds(i1, 1), :][:, lanes] for h in range(PEER_HEADS)]
            e1 = [jnp.exp(s1[h] - m1_ref[h:h + 1, lanes]) for h in range(PEER_HEADS)]
            thr = [jnp.where(live, thr_ref[h:h + 1, lanes], jnp.inf) + after for h in range(PEER_HEADS)]
            halves = []
            for r in range(2):
                rows = slice(r * half, (r + 1) * half)
                w = jnp.zeros((half, LANES), F32)
                for h in range(PEER_HEADS):
                    sel = (s_ref[2 * h + 1, rows, lanes] + s1[h]) >= thr[h]
                    w = w + jnp.where(sel, e2_sc[h, rows, lanes] * e1[h], 0.0)
                a = a_ref[q * PEER_KEYS + r * half:q * PEER_KEYS + (r + 1) * half, lanes]
                halves.append(w * (0.5 * a * (1.0 + lax.erf(a * (2.0 ** -0.5)))))
            blk = jnp.concatenate(halves, axis=0)
            wa_ref[lanes, q * PEER_KEYS:(q + 1) * PEER_KEYS] = blk.T.astype(wa_ref.dtype)
        return run

    return [block(q, l) for q in range(ts // PEER_KEYS) for l in range(tm // LANES)]


def _interleave(vpu_tasks, mxu_tasks):
    n_v, n_m = len(vpu_tasks), len(mxu_tasks)
    done_m = 0
    after = jnp.zeros((1, LANES), F32)
    for i, task in enumerate(vpu_tasks):
        while done_m * n_v < (i + 1) * n_m:
            row = mxu_tasks[done_m]()
            after = jnp.where(row != row, row, 0.0)
            done_m += 1
        task(after)


def _peer_dense_kernel(ht_ref, u_ref, v_ref, s_ref, thr_ref, m1_ref, m2_ref, rz_ref, x1_ref, g2_ref, o_ref,
                       a0, a1, wa0, wa1, e2_sc):
    k = pl.program_id(1)
    last = pl.num_programs(1) - 1
    ts = a0.shape[0]

    @pl.when(k == 0)
    def _():
        o_ref[...] = jnp.zeros(o_ref.shape, F32)
        wa0[...] = jnp.zeros(wa0.shape, wa0.dtype)
        a1[...] = jnp.zeros(a1.shape, F32)
        for h in range(PEER_HEADS):
            e2_sc[h] = jnp.exp(s_ref[2 * h + 1] - m2_ref[h:h + 1, :]) * rz_ref[h:h + 1, :]

    d = o_ref.shape[1]
    cw = min(d, MXU_COLS)
    rh = ts // 2

    def act_chunk(a_ref, base, c):
        def run():
            r = _dot(u_ref[base + c * rh:base + (c + 1) * rh, :], ht_ref[...])
            a_ref[c * rh:(c + 1) * rh, :] = r
            return r[rh - 1:rh, :LANES]
        return run

    def val_chunk(wa_ref, base, c):
        def run():
            r = _dot(wa_ref[...], v_ref[base:base + ts, c * cw:(c + 1) * cw])
            o_ref[:, c * cw:(c + 1) * cw] += r
            return r[r.shape[0] - 1:, :LANES]
        return run

    weights = lambda tile, live, a_ref, wa_ref: _peer_weighted_acts(tile, live, a_ref, wa_ref, s_ref, thr_ref,
                                                                    m1_ref, e2_sc)
    _interleave(weights(2 * k - 1, k >= 1, a1, wa1),
                [act_chunk(a0, 0, c) for c in range(2)] + [val_chunk(wa0, 0, c) for c in range(d // cw)])
    _interleave(weights(2 * k, k < last, a0, wa0),
                [act_chunk(a1, ts, c) for c in range(2)] + [val_chunk(wa1, ts, c) for c in range(d // cw)])

    @pl.when(k == last)
    def _():
        o_ref[...] = x1_ref[...] + g2_ref[...] * o_ref[...]


def _peer_dense(h2t, u16, v16, s_t, thr, m1, m2, rz, x1, ga2, tm, ts, rows_per_group):
    d, n = h2t.shape
    n_exp = u16.shape[0]
    n_hc = s_t.shape[0]
    n_k = n_exp // (2 * ts)
    tok = lambda i, k: (0, i)
    small_spec = pl.BlockSpec((PEER_HEADS, tm), tok)
    g2_spec = pl.BlockSpec((None,) + ga2.shape[1:], lambda i, k: ((i * tm) // rows_per_group, 0, 0))
    return pl.pallas_call(
        _peer_dense_kernel,
        grid=(n // tm, n_k + 1),
        in_specs=[pl.BlockSpec((d, tm), tok),
                  pl.BlockSpec((2 * ts, d), lambda i, k: (jnp.minimum(k, n_k - 1), 0)),
                  pl.BlockSpec((2 * ts, d), lambda i, k: (jnp.maximum(k - 1, 0), 0)),
                  pl.BlockSpec((n_hc, PEER_KEYS, tm), lambda i, k: (0, 0, i))] + [small_spec] * 4
        + [pl.BlockSpec((tm, d), lambda i, k: (i, 0)), g2_spec],
        out_specs=pl.BlockSpec((tm, d), lambda i, k: (i, 0)),
        out_shape=jax.ShapeDtypeStruct((n, d), F32),
        scratch_shapes=[pltpu.VMEM((ts, tm), F32), pltpu.VMEM((ts, tm), F32),
                        pltpu.VMEM((tm, ts), BF16), pltpu.VMEM((tm, ts), BF16),
                        pltpu.VMEM((PEER_HEADS, PEER_KEYS, tm), F32)],
        compiler_params=_cparams("arbitrary", "arbitrary"), name="peer_dense",
    )(h2t, u16, v16, s_t, thr, m1, m2, rz, x1, ga2)


def _split_w_in(w_in, d_model):
    sizes = (A_HEADS * HEAD_DIM, A_KV_HEADS * HEAD_DIM, A_KV_HEADS * HEAD_DIM, A_HEADS,
             B_HEADS * HEAD_DIM, B_KV_HEADS * HEAD_DIM, B_KV_HEADS * HEAD_DIM,
             IDX_HEADS * IDX_DIM, IDX_DIM, IDX_HEADS, d_model, d_model)
    offs = [0]
    for s in sizes:
        offs.append(offs[-1] + s)
    cols = lambda a, b: w_in[:, offs[a]:offs[b]].astype(BF16)
    padw = lambda a: jnp.pad(w_in[:, offs[a]:offs[a + 1]], ((0, 0), (0, LANES - sizes[a]))).astype(BF16)
    return dict(a=cols(0, 3), b=cols(4, 7), qi=cols(7, 8),
                misc=jnp.concatenate([padw(3), padw(8), padw(9)], axis=1),
                gate_a=cols(10, 11), gate_b=cols(11, 12))


def _front(x2, mods, pos, w, p, tm, rows_per_group):
    sh1, sc1 = mods[0], mods[1]
    h = _prenorm(x2, p["g_norm1"], sc1, sh1, tm, rows_per_group)
    tab128, half128 = _rope_tables(pos, HEAD_DIM)
    tab64, half64 = _rope_tables(pos, IDX_DIM)
    gain_a = jnp.concatenate([jnp.tile(p["g_qn_a"], A_HEADS), jnp.tile(p["g_kn_a"], A_KV_HEADS)]).reshape(1, -1)
    gain_b = jnp.concatenate([jnp.tile(p["g_qn_b"], B_HEADS), jnp.tile(p["g_kn_b"], B_KV_HEADS)]).reshape(1, -1)
    q_a, k_a, v_a, ka16, va16 = _qkv_proj(h, w["a"], gain_a, None, 0, tm, A_HEADS, A_KV_HEADS)
    q_b, k_b, v_b, kb16, vb16 = _qkv_proj(h, w["b"], gain_b, tab128, half128, tm, B_HEADS, B_KV_HEADS)
    q_i = _qidx_proj(h, w["qi"], tab64, half64, tm)
    logf, k_i, w_i = _misc_proj(h, w["misc"], p["b_fgate"], tab64, half64, tm)
    gate_a = _gate_proj(h, w["gate_a"], tm)
    gate_b = _gate_proj(h, w["gate_b"], tm)
    return dict(q_a=q_a, k_a=k_a, v_a=v_a, ka16=ka16, va16=va16, logf=logf,
                q_b=q_b, k_b=k_b, v_b=v_b, kb16=kb16, vb16=vb16,
                q_i=q_i, k_i=k_i, w_i=w_i, gate_a=gate_a, gate_b=gate_b)


def _back(x2, f, o_a, o_b, mods, p, tm, rows_per_group, tm_peer, te):
    ga1, sh2, sc2, ga2 = mods[2], mods[3], mods[4], mods[5]
    merged = _merge(o_a, o_b, f["gate_a"], f["gate_b"], p["w_branch_a"], p["w_branch_b"], tm)
    x1, h2t = _outproj(x2, merged, p["w_out"], ga1, p["g_norm2"], sc2, sh2, tm, rows_per_group)
    s_t, thr, m1, m2, rz = _peer_query(h2t, p["wq_t"], p["subkeys"], tm_peer)
    return _peer_dense(h2t, p["u16"], p["v16"], s_t, thr, m1, m2, rz, x1, ga2, tm_peer, te, rows_per_group)


def _prompt_group(x, ada, w, p):
    bsz, t, d = x.shape
    tm = _tile(t, 512)
    mods = [ada[:, i].reshape(bsz, 1, d) for i in range(N_ADA)]
    x2 = x.reshape(bsz * t, d)
    f = _front(x2, mods, jnp.arange(t), w, p, tm, t)
    fk = _cumsum_lanes(f["logf"].reshape(bsz, t, A_HEADS).transpose(0, 2, 1))
    o_a = _fox_prompt(f["q_a"], f["ka16"], f["va16"], fk.transpose(0, 2, 1), fk, bsz, t)
    o_b = _dsa_prompt(f["q_b"], f["q_i"], f["w_i"], f["k_i"], f["kb16"], f["vb16"], bsz, t)
    y = _back(x2, f, o_a, o_b, mods, p, tm, t, _tile(bsz * t, 512), 512)
    return y.reshape(bsz, t, d), f


def _new_key_bias(base, n_heads, n_kv):
    bsz, n_tok = base.shape[0], base.shape[1]
    head_ok = (jnp.arange(n_heads)[:, None] // (n_heads // n_kv)) == jnp.arange(n_kv)[None, :]
    full = jnp.where(head_ok[None, None, :, None, :], base[..., None], NEG)
    full = full.reshape(bsz, n_tok * n_heads, n_tok * n_kv)
    return jnp.pad(full, ((0, 0), (0, 0), (0, PAGE_SIZE - n_tok * n_kv)), constant_values=NEG)


def _pad_rows(a, rows):
    return jnp.pad(a, ((0, 0), (0, rows - a.shape[1]), (0, 0)))


def _sample_group(x, ada, w, p, caches, page_table):
    bsz, n_tok, d = x.shape
    n = bsz * n_tok
    ck_a, cv_a, clf, ck_b, cv_b, cki = caches
    n_pages = page_table.shape[1]
    past = n_pages * PAGE_SIZE
    tm = _tile(n, 512)
    mods = [jnp.repeat(ada[:, i], n_tok, axis=0).reshape(n // tm, tm, d) for i in range(N_ADA)]
    x2 = x.reshape(n, d)
    pos = jnp.tile(past + jnp.arange(n_tok), bsz)
    f = _front(x2, mods, pos, w, p, tm, tm)
    rep_attn, rep_small = _rep(n_pages, 16), _rep(n_pages, 32)
    t_idx = jnp.arange(n_tok)
    causal = t_idx[None, :] <= t_idx[:, None]

    def rows128(a, n_kv):
        return _pad_rows(a.reshape(bsz, n_tok * n_kv, HEAD_DIM), PAGE_SIZE)

    lf_new = jnp.pad(f["logf"].reshape(bsz, n_tok, A_HEADS).transpose(0, 2, 1),
                     ((0, 0), (0, 0), (0, PAGE_SIZE - n_tok)))
    neg_fp, fn_t = _fscan(page_table, clf, lf_new, rep_small, A_KV_HEADS)
    fn = fn_t[:, :, :n_tok].transpose(0, 2, 1)
    base = jnp.where(causal[None, :, None, :], -fn.transpose(0, 2, 1)[:, None, :, :], NEG)
    o_a = _paged_attn(page_table, ck_a, cv_a, f["q_a"].reshape(bsz, n_tok * A_HEADS, HEAD_DIM),
                      neg_fp, fn.reshape(bsz, n_tok * A_HEADS, 1),
                      rows128(f["k_a"], A_KV_HEADS), rows128(f["v_a"], A_KV_HEADS),
                      _new_key_bias(base, A_HEADS, A_KV_HEADS), rep_attn, A_KV_HEADS, A_HEADS, True)
    o_a = o_a.reshape(n, A_HEADS * HEAD_DIM).astype(BF16)

    qi = f["q_i"].reshape(IDX_HEADS, bsz, n_tok, IDX_DIM).transpose(1, 2, 0, 3).reshape(bsz, n_tok * IDX_HEADS, IDX_DIM)
    wi = f["w_i"].reshape(bsz, n_tok * IDX_HEADS, 1)
    key_bias, nbm = _sample_index(page_table, cki, qi, wi,
                                  _pad_rows(f["k_i"].reshape(bsz, n_tok, IDX_DIM), PAGE_SIZE),
                                  rep_small, n_tok, B_KV_HEADS)
    base = jnp.broadcast_to(nbm[:, :n_tok, None, :n_tok], (bsz, n_tok, B_HEADS, n_tok))
    o_b = _paged_attn(page_table, ck_b, cv_b, f["q_b"].reshape(bsz, n_tok * B_HEADS, HEAD_DIM),
                      key_bias, jnp.zeros((bsz, n_tok * B_HEADS, 1), F32),
                      rows128(f["k_b"], B_KV_HEADS), rows128(f["v_b"], B_KV_HEADS),
                      _new_key_bias(base, B_HEADS, B_KV_HEADS), rep_attn, B_KV_HEADS, B_HEADS, False)
    o_b = o_b.reshape(n, B_HEADS * HEAD_DIM).astype(BF16)

    y = _back(x2, f, o_a, o_b, mods, p, tm, tm, _tile(n, 512), 512)
    return y.reshape(bsz, n_tok, d), f


def _layer_outputs(f, bsz, t):
    return (f["k_a"].reshape(bsz, t, A_KV_HEADS, HEAD_DIM), f["v_a"].reshape(bsz, t, A_KV_HEADS, HEAD_DIM),
            f["logf"].reshape(bsz, t, A_HEADS),
            f["k_b"].reshape(bsz, t, B_KV_HEADS, HEAD_DIM), f["v_b"].reshape(bsz, t, B_KV_HEADS, HEAD_DIM),
            f["k_i"].reshape(bsz, t, IDX_DIM))


def kernel(x_prompt, x_sample, cache_fox_k, cache_fox_v, cache_fox_logf, cache_dsa_k, cache_dsa_v, cache_idx_k, page_table, c_prompt, c_sample, w_ada, b_ada, g_norm1, g_norm2, w_in, b_fgate, g_qn_a, g_kn_a, g_qn_b, g_kn_b, w_branch_a, w_branch_b, w_out, w_peer_q, peer_subkeys, peer_u, peer_v):
    depth = w_ada.shape[0]
    bsz, t, d = x_prompt.shape
    bs, ts, _ = x_sample.shape
    xp, xs = x_prompt, x_sample
    rows_p, rows_s = [], []
    n_c = bsz + bs
    c_all = jnp.pad(jnp.concatenate([c_prompt, c_sample], axis=0), ((0, (-n_c) % 8), (0, 0)))
    n_pool = cache_fox_k.shape[1]
    pools = (cache_fox_k.reshape(-1, HEAD_DIM), cache_fox_v.reshape(-1, HEAD_DIM),
             cache_fox_logf.transpose(0, 1, 3, 2).reshape(depth * n_pool, A_HEADS, PAGE_SIZE),
             cache_dsa_k.reshape(-1, HEAD_DIM), cache_dsa_v.reshape(-1, HEAD_DIM),
             cache_idx_k.reshape(depth * n_pool, PAGE_SIZE, IDX_DIM))
    for l in range(depth):
        ada = _ada(c_all, w_ada[l], b_ada[l]).reshape(c_all.shape[0], N_ADA, d)
        w = _split_w_in(w_in[l], d)
        p = dict(g_norm1=g_norm1[l], g_norm2=g_norm2[l], b_fgate=b_fgate[l], g_qn_a=g_qn_a[l], g_kn_a=g_kn_a[l],
                 g_qn_b=g_qn_b[l], g_kn_b=g_kn_b[l],
                 w_branch_a=w_branch_a[l].astype(BF16), w_branch_b=w_branch_b[l].astype(BF16),
                 w_out=w_out[l].astype(BF16), wq_t=w_peer_q[l].T.astype(BF16),
                 subkeys=peer_subkeys[l].reshape(PEER_HEADS * 2, PEER_KEYS, -1).astype(BF16),
                 u16=peer_u[l].astype(BF16), v16=peer_v[l].astype(BF16))
        xp, f_p = _prompt_group(xp, ada[:bsz], w, p)
        xs, f_s = _sample_group(xs, ada[bsz:n_c], w, p, pools, page_table + l * n_pool)
        rows_p.append(_layer_outputs(f_p, bsz, t))
        rows_s.append(_layer_outputs(f_s, bs, ts))
    outs_p = [jnp.stack([r[i] for r in rows_p], axis=0) for i in range(6)]
    outs_s = [jnp.stack([r[i] for r in rows_s], axis=0) for i in range(6)]
    return (xp, xs, *outs_p, *outs_s)
```

```python
import functools

import jax
import jax.numpy as jnp
from jax import lax
from jax.experimental import pallas as pl
from jax.experimental.pallas import tpu as pltpu

F32 = jnp.float32
BF16 = jnp.bfloat16
I32 = jnp.int32

HEAD_DIM = 128
A_HEADS = 8
A_KV_HEADS = 4
B_HEADS = 8
B_KV_HEADS = 4
IDX_HEADS = 16
IDX_DIM = 64
ROPE_THETA = 500000.0
ROPE_FRAC = 4
TOPK_MAX = 256
PEER_KEYS = 128
PEER_HEADS = 8
PEER_TOPK = 16
PAGE_SIZE = 128
EPS = 1e-6
ATTN_SCALE = HEAD_DIM ** -0.5
N_ADA = 6
LANES = 128
MXU_COLS = 256
NEG = -1e30
LOWEST = -1.0e38
BISECT_ITERS = 40
VMEM_LIMIT = 56 * 1024 * 1024


def _cparams(*sem):
    return pltpu.CompilerParams(dimension_semantics=sem, vmem_limit_bytes=VMEM_LIMIT)


def _tile(n, pref, mult=LANES):
    best = None
    t = mult
    while t <= min(n, pref):
        if n % t == 0:
            best = t
        t += mult
    return best if best is not None else n


def _dot(a, b):
    return jnp.dot(a, b, preferred_element_type=F32)


def _dot_nt(a, b):
    return lax.dot_general(a, b, (((1,), (1,)), ((), ())), preferred_element_type=F32)


def _sigmoid(x):
    return 1.0 / (1.0 + jnp.exp(-x))


def _rep(n, pref):
    return max(r for r in range(1, min(n, pref) + 1) if n % r == 0)


def _ada_kernel(c_ref, w_ref, b_ref, o_ref):
    c = c_ref[...]
    s = c * _sigmoid(c)
    o_ref[...] = _dot(s.astype(BF16), w_ref[...].astype(BF16)) + b_ref[...]


def _ada(c_all, w_ada, b_ada):
    m, d = c_all.shape
    n = w_ada.shape[1]
    tn = _tile(n, 1024)
    return pl.pallas_call(
        _ada_kernel,
        grid=(n // tn,),
        in_specs=[pl.BlockSpec((m, d), lambda j: (0, 0)),
                  pl.BlockSpec((d, tn), lambda j: (0, j)),
                  pl.BlockSpec((1, tn), lambda j: (0, j))],
        out_specs=pl.BlockSpec((m, tn), lambda j: (0, j)),
        out_shape=jax.ShapeDtypeStruct((m, n), F32),
        compiler_params=_cparams("arbitrary"),
        name="ada",
    )(c_all, w_ada, b_ada.reshape(1, n))


def _rms_mod(x, g, sc, sh):
    y = x * lax.rsqrt(jnp.mean(x * x, axis=-1, keepdims=True) + EPS)
    return (y * g) * (1.0 + sc) + sh


def _prenorm_kernel(x_ref, g_ref, sc_ref, sh_ref, o_ref):
    o_ref[...] = _rms_mod(x_ref[...], g_ref[...], sc_ref[...], sh_ref[...]).astype(o_ref.dtype)


def _mod_spec(mod, tm, rows_per_group):
    r, d = mod.shape[1], mod.shape[2]
    return pl.BlockSpec((None, r, d), lambda i: ((i * tm) // rows_per_group, 0, 0))


def _prenorm(x, g, sc, sh, tm, rows_per_group):
    n, d = x.shape
    return pl.pallas_call(
        _prenorm_kernel,
        grid=(n // tm,),
        in_specs=[pl.BlockSpec((tm, d), lambda i: (i, 0)),
                  pl.BlockSpec((1, d), lambda i: (0, 0)),
                  _mod_spec(sc, tm, rows_per_group),
                  _mod_spec(sh, tm, rows_per_group)],
        out_specs=pl.BlockSpec((tm, d), lambda i: (i, 0)),
        out_shape=jax.ShapeDtypeStruct((n, d), BF16),
        compiler_params=_cparams("arbitrary"),
        name="prenorm",
    )(x, g.reshape(1, d), sc, sh)


def _rope_tables(pos, head_dim):
    rd = head_dim // ROPE_FRAC
    half = rd // 2
    inv = ROPE_THETA ** (-jnp.arange(half, dtype=F32) / half)
    ang = pos.astype(F32)[:, None] * inv[None, :]
    cos, sin = jnp.cos(ang), jnp.sin(ang)
    t = pos.shape[0]
    rest = head_dim - rd
    c = jnp.concatenate([cos, cos, jnp.ones((t, rest), F32)], axis=-1)
    sa = jnp.concatenate([-sin, jnp.zeros((t, half + rest), F32)], axis=-1)
    sb = jnp.concatenate([jnp.zeros((t, half), F32), sin, jnp.zeros((t, rest), F32)], axis=-1)
    reps = LANES // head_dim
    return tuple(jnp.tile(a, (1, reps)) for a in (c, sa, sb)), half


def _rope(y, c, sa, sb, half):
    return y * c + pltpu.roll(y, LANES - half, 1) * sa + pltpu.roll(y, half, 1) * sb


def _head_norm(chunk, gain):
    y = chunk * lax.rsqrt(jnp.mean(chunk * chunk, axis=-1, keepdims=True) + EPS)
    return y * gain


def _qkv_kernel(*refs, n_q, n_k, rope_half):
    h_ref, w_ref, gain_ref = refs[:3]
    pos = 3
    if rope_half:
        c_ref, sa_ref, sb_ref = refs[3:6]
        pos = 6
    q_ref, k_ref, v_ref, kb_ref, vb_ref = refs[pos:pos + 5]
    tm = h_ref.shape[0]
    z = _dot(h_ref[...], w_ref[...])
    for ch in range(n_q + n_k):
        sl = slice(ch * LANES, (ch + 1) * LANES)
        y = _head_norm(z[:, sl], gain_ref[:, sl])
        if rope_half:
            y = _rope(y, c_ref[...], sa_ref[...], sb_ref[...], rope_half)
        if ch < n_q:
            q_ref[:, sl] = (y * ATTN_SCALE).astype(q_ref.dtype)
        else:
            j = ch - n_q
            k_ref[pl.ds(j, tm, stride=n_k), :] = y
            kb_ref[:, j * LANES:(j + 1) * LANES] = y.astype(BF16)
    for j in range(n_k):
        v = z[:, (n_q + n_k + j) * LANES:(n_q + n_k + j + 1) * LANES]
        v_ref[pl.ds(j, tm, stride=n_k), :] = v
        vb_ref[:, j * LANES:(j + 1) * LANES] = v.astype(BF16)


def _qkv_proj(h, w, gain, tables, rope_half, tm, n_q, n_k):
    n, d = h.shape
    wq, wk = n_q * LANES, n_k * LANES
    width = w.shape[1]
    row = lambda i: (i, 0)
    const = lambda i: (0, 0)
    in_specs = [pl.BlockSpec((tm, d), row), pl.BlockSpec((d, width), const),
                pl.BlockSpec((1, wq + wk), const)]
    args = [h, w, gain]
    if rope_half:
        nt = tables[0].shape[0] // tm
        for t in tables:
            in_specs.append(pl.BlockSpec((tm, LANES), lambda i: (i % nt, 0)))
            args.append(t)
    out_shape = ([jax.ShapeDtypeStruct((n, wq), BF16)] + [jax.ShapeDtypeStruct((n * n_k, LANES), F32)] * 2
                 + [jax.ShapeDtypeStruct((n, wk), BF16)] * 2)
    out_specs = ([pl.BlockSpec((tm, wq), row)] + [pl.BlockSpec((tm * n_k, LANES), row)] * 2
                 + [pl.BlockSpec((tm, wk), row)] * 2)
    return pl.pallas_call(
        functools.partial(_qkv_kernel, n_q=n_q, n_k=n_k, rope_half=rope_half),
        grid=(n // tm,), in_specs=in_specs, out_specs=out_specs, out_shape=out_shape,
        compiler_params=_cparams("arbitrary"), name="qkv_proj",
    )(*args)


def _qidx_kernel(h_ref, w_ref, c_ref, sa_ref, sb_ref, o_ref, *, half):
    z = _dot(h_ref[...], w_ref[...])
    for ch in range(IDX_HEADS // 2):
        y = _rope(z[:, ch * LANES:(ch + 1) * LANES], c_ref[...], sa_ref[...], sb_ref[...], half)
        o_ref[2 * ch] = y[:, :IDX_DIM].astype(o_ref.dtype)
        o_ref[2 * ch + 1] = y[:, IDX_DIM:].astype(o_ref.dtype)


def _qidx_proj(h, w, tables, half, tm):
    n, d = h.shape
    nt = tables[0].shape[0] // tm
    tab = pl.BlockSpec((tm, LANES), lambda i: (i % nt, 0))
    return pl.pallas_call(
        functools.partial(_qidx_kernel, half=half),
        grid=(n // tm,),
        in_specs=[pl.BlockSpec((tm, d), lambda i: (i, 0)),
                  pl.BlockSpec((d, IDX_HEADS * IDX_DIM), lambda i: (0, 0)), tab, tab, tab],
        out_specs=pl.BlockSpec((IDX_HEADS, tm, IDX_DIM), lambda i: (0, i, 0)),
        out_shape=jax.ShapeDtypeStruct((IDX_HEADS, n, IDX_DIM), BF16),
        compiler_params=_cparams("arbitrary"), name="qidx_proj",
    )(h, w, *tables)


def _misc_kernel(h_ref, w_ref, bf_ref, c_ref, sa_ref, sb_ref, logf_ref, ki_ref, wi_ref, *, half):
    z = _dot(h_ref[...], w_ref[...])
    f = z[:, :A_HEADS] + bf_ref[...]
    logf_ref[...] = jnp.minimum(f, 0.0) - jnp.log1p(jnp.exp(-jnp.abs(f)))
    y = _rope(z[:, LANES:2 * LANES], c_ref[...], sa_ref[...], sb_ref[...], half)
    ki_ref[...] = y[:, :IDX_DIM]
    wi_ref[...] = z[:, 2 * LANES:2 * LANES + IDX_HEADS]


def _misc_proj(h, w, b_fgate, tables, half, tm):
    n, d = h.shape
    nt = tables[0].shape[0] // tm
    tab = pl.BlockSpec((tm, LANES), lambda i: (i % nt, 0))
    row = lambda i: (i, 0)
    return pl.pallas_call(
        functools.partial(_misc_kernel, half=half),
        grid=(n // tm,),
        in_specs=[pl.BlockSpec((tm, d), row), pl.BlockSpec((d, 3 * LANES), lambda i: (0, 0)),
                  pl.BlockSpec((1, A_HEADS), lambda i: (0, 0)), tab, tab, tab],
        out_specs=[pl.BlockSpec((tm, A_HEADS), row), pl.BlockSpec((tm, IDX_DIM), row),
                   pl.BlockSpec((tm, IDX_HEADS), row)],
        out_shape=[jax.ShapeDtypeStruct((n, A_HEADS), F32), jax.ShapeDtypeStruct((n, IDX_DIM), F32),
                   jax.ShapeDtypeStruct((n, IDX_HEADS), F32)],
        compiler_params=_cparams("arbitrary"), name="misc_proj",
    )(h, w, b_fgate.reshape(1, A_HEADS), *tables)


def _gate_kernel(h_ref, w_ref, o_ref):
    o_ref[...] = _sigmoid(_dot(h_ref[...], w_ref[...]))


def _gate_proj(h, w, tm):
    n, d = h.shape
    width = w.shape[1]
    return pl.pallas_call(
        _gate_kernel, grid=(n // tm,),
        in_specs=[pl.BlockSpec((tm, d), lambda i: (i, 0)), pl.BlockSpec((d, width), lambda i: (0, 0))],
        out_specs=pl.BlockSpec((tm, width), lambda i: (i, 0)),
        out_shape=jax.ShapeDtypeStruct((n, width), F32),
        compiler_params=_cparams("arbitrary"), name="gate_proj",
    )(h, w)


def _cumsum_kernel(x_ref, o_ref, *, tc):
    t = x_ref.shape[-1]
    r = lax.broadcasted_iota(I32, (tc, tc), 0)
    c = lax.broadcasted_iota(I32, (tc, tc), 1)
    tri = jnp.where(r <= c, 1.0, 0.0).astype(F32)
    carry = jnp.zeros((x_ref.shape[0], 1), F32)
    for blk in range(t // tc):
        sl = slice(blk * tc, (blk + 1) * tc)
        cs = jnp.dot(x_ref[:, sl], tri, precision=lax.Precision.HIGHEST, preferred_element_type=F32) + carry
        o_ref[:, sl] = cs
        carry = cs[:, tc - 1:tc]


def _cumsum_lanes(x):
    b, h, t = x.shape
    tc = _tile(t, 512)
    return pl.pallas_call(
        functools.partial(_cumsum_kernel, tc=tc), grid=(b,),
        in_specs=[pl.BlockSpec((None, h, t), lambda i: (i, 0, 0))],
        out_specs=pl.BlockSpec((None, h, t), lambda i: (i, 0, 0)),
        out_shape=jax.ShapeDtypeStruct((b, h, t), F32),
        compiler_params=_cparams("arbitrary"), name="cumsum",
    )(x)


def _online_update(s, v16, m_ref, l_ref, acc_ref):
    m_prev = m_ref[...]
    m_new = jnp.maximum(m_prev, jnp.max(s, axis=-1, keepdims=True))
    alpha = jnp.exp(m_prev - m_new)
    p = jnp.exp(s - m_new)
    l_ref[...] = alpha * l_ref[...] + jnp.sum(p, axis=-1, keepdims=True)
    acc_ref[...] = alpha * acc_ref[...] + _dot(p.astype(BF16), v16)
    m_ref[...] = m_new


def _fox_kernel(q_ref, k_ref, v_ref, fq_ref, fk_ref, o_ref, m_sc, l_sc, acc_sc, *, tq, tk, group):
    i = pl.program_id(1)
    kk = pl.program_id(2)
    n_heads = q_ref.shape[1] // LANES

    @pl.when(kk == 0)
    def _():
        m_sc[...] = jnp.full(m_sc.shape, NEG, F32)
        l_sc[...] = jnp.zeros(l_sc.shape, F32)
        acc_sc[...] = jnp.zeros(acc_sc.shape, F32)

    def attend(masked):
        if masked:
            row = i * tq + lax.broadcasted_iota(I32, (tq, tk), 0)
            col = kk * tk + lax.broadcasted_iota(I32, (tq, tk), 1)
            causal = col <= row
        for h in range(n_heads):
            kv = slice((h // group) * LANES, (h // group + 1) * LANES)
            s = _dot_nt(q_ref[:, h * LANES:(h + 1) * LANES], k_ref[:, kv])
            s = s + fq_ref[:, h:h + 1] - fk_ref[h:h + 1, :]
            if masked:
                s = jnp.where(causal, s, NEG)
            _online_update(s, v_ref[:, kv], m_sc.at[h], l_sc.at[h], acc_sc.at[h])

    first_row, last_key = i * tq, kk * tk + tk - 1
    pl.when(last_key <= first_row)(lambda: attend(False))
    pl.when(jnp.logical_and(last_key > first_row, kk * tk <= first_row + tq - 1))(lambda: attend(True))

    @pl.when(kk == pl.num_programs(2) - 1)
    def _():
        for h in range(n_heads):
            o_ref[:, h * LANES:(h + 1) * LANES] = (acc_sc[h] / l_sc[h]).astype(o_ref.dtype)


def _fox_prompt(q, k, v, fq, fk, bsz, t):
    tq = _tile(t, 256)
    tk = _tile(t, 1024)
    nq, nk = t // tq, t // tk
    kvw = A_KV_HEADS * LANES
    last_k = lambda i, kk: jnp.minimum(kk, (i * tq + tq - 1) // tk)
    return pl.pallas_call(
        functools.partial(_fox_kernel, tq=tq, tk=tk, group=A_HEADS // A_KV_HEADS),
        grid=(bsz, nq, nk),
        in_specs=[pl.BlockSpec((tq, A_HEADS * LANES), lambda b, i, kk: (b * nq + i, 0)),
                  pl.BlockSpec((tk, kvw), lambda b, i, kk: (b * nk + last_k(i, kk), 0)),
                  pl.BlockSpec((tk, kvw), lambda b, i, kk: (b * nk + last_k(i, kk), 0)),
                  pl.BlockSpec((None, tq, A_HEADS), lambda b, i, kk: (b, i, 0)),
                  pl.BlockSpec((None, A_HEADS, tk), lambda b, i, kk: (b, 0, last_k(i, kk)))],
        out_specs=pl.BlockSpec((tq, A_HEADS * LANES), lambda b, i, kk: (b * nq + i, 0)),
        out_shape=jax.ShapeDtypeStruct((bsz * t, A_HEADS * LANES), BF16),
        scratch_shapes=[pltpu.VMEM((A_HEADS, tq, 1), F32), pltpu.VMEM((A_HEADS, tq, 1), F32),
                        pltpu.VMEM((A_HEADS, tq, LANES), F32)],
        compiler_params=_cparams("arbitrary", "arbitrary", "arbitrary"),
        name="fox_prompt",
    )(q, k, v, fq, fk)


def _kth_value(count_ge, lo, hi, c_lo, k):
    def cond(st):
        return jnp.logical_and(st[0] < BISECT_ITERS, jnp.max(st[3]) > k)

    def body(st):
        it, lo, hi, c_lo = st
        mid = 0.5 * lo + 0.5 * hi
        c = count_ge(mid)
        ge = c >= k
        return it + 1, jnp.where(ge, mid, lo), jnp.where(ge, hi, mid), jnp.where(ge, c, c_lo)

    return lax.while_loop(cond, body, (jnp.int32(0), lo, hi, c_lo))[1]


def _dsa_kernel(qb_ref, qi_ref, wi_ref, ki_ref, kb_ref, vb_ref, o_ref, key_sc, m_sc, l_sc, acc_sc,
                *, tq, tk, topk):
    i = pl.program_id(1)
    nch = (i * tq + tq + tk - 1) // tk
    row = i * tq + lax.broadcasted_iota(I32, (tq, 1), 0)

    def scores(c, carry):
        mx, mn = carry
        kc = ki_ref[pl.ds(pl.multiple_of(c * tk, tk), tk), :].astype(BF16)
        acc = jnp.zeros((tq, tk), F32)
        for h in range(IDX_HEADS):
            acc = acc + wi_ref[:, h:h + 1] * jnp.maximum(_dot_nt(qi_ref[h], kc), 0.0)
        valid = (c * tk + lax.broadcasted_iota(I32, (1, tk), 1)) <= row
        key_sc[c] = jnp.where(valid, acc, -jnp.inf)
        mx = jnp.maximum(mx, jnp.max(jnp.where(valid, acc, -jnp.inf), axis=-1, keepdims=True))
        mn = jnp.minimum(mn, jnp.min(jnp.where(valid, acc, jnp.inf), axis=-1, keepdims=True))
        return mx, mn

    mx, mn = lax.fori_loop(0, nch, scores, (jnp.full((tq, 1), -jnp.inf, F32), jnp.full((tq, 1), jnp.inf, F32)))

    def count_ge(cand):
        def cbody(c, cnt):
            ge = jnp.where(key_sc[c] >= cand, 1.0, 0.0)
            for s in range(tk // LANES):
                cnt = cnt + ge[:, s * LANES:(s + 1) * LANES]
            return cnt
        cnt = lax.fori_loop(0, nch, cbody, jnp.zeros((tq, LANES), F32))
        return jnp.sum(cnt, axis=-1, keepdims=True)

    many = row >= topk
    lo0 = jnp.where(many, mn, LOWEST)
    hi0 = jnp.where(many, mx, LOWEST)
    c0 = jnp.where(many, (row + 1).astype(F32), 0.0)
    thr = _kth_value(count_ge, lo0, hi0, c0, float(topk))

    m_sc[...] = jnp.full(m_sc.shape, NEG, F32)
    l_sc[...] = jnp.zeros(l_sc.shape, F32)
    acc_sc[...] = jnp.zeros(acc_sc.shape, F32)

    def attend(c, carry):
        bias = jnp.where(key_sc[c] >= thr, 0.0, NEG)
        rows = pl.ds(pl.multiple_of(c * tk, tk), tk)
        for h in range(B_HEADS):
            j = h // (B_HEADS // B_KV_HEADS)
            k16 = kb_ref[rows, j * LANES:(j + 1) * LANES]
            v16 = vb_ref[rows, j * LANES:(j + 1) * LANES]
            s = _dot_nt(qb_ref[:, h * LANES:(h + 1) * LANES], k16) + bias
            _online_update(s, v16, m_sc.at[h], l_sc.at[h], acc_sc.at[h])
        return carry

    lax.fori_loop(0, nch, attend, 0)
    for h in range(B_HEADS):
        o_ref[:, h * LANES:(h + 1) * LANES] = (acc_sc[h] / l_sc[h]).astype(o_ref.dtype)


def _dsa_prompt(qb, qi, wi, ki, kb16, vb16, bsz, t):
    tq = _tile(t, 256)
    tk = _tile(t, 1024)
    nq = t // tq
    topk = min(TOPK_MAX, t // 4)
    kvw = B_KV_HEADS * LANES
    return pl.pallas_call(
        functools.partial(_dsa_kernel, tq=tq, tk=tk, topk=topk),
        grid=(bsz, nq),
        in_specs=[pl.BlockSpec((tq, B_HEADS * LANES), lambda b, i: (b * nq + i, 0)),
                  pl.BlockSpec((IDX_HEADS, tq, IDX_DIM), lambda b, i: (0, b * nq + i, 0)),
                  pl.BlockSpec((tq, IDX_HEADS), lambda b, i: (b * nq + i, 0)),
                  pl.BlockSpec((t, IDX_DIM), lambda b, i: (b, 0)),
                  pl.BlockSpec((t, kvw), lambda b, i: (b, 0)),
                  pl.BlockSpec((t, kvw), lambda b, i: (b, 0))],
        out_specs=pl.BlockSpec((tq, B_HEADS * LANES), lambda b, i: (b * nq + i, 0)),
        out_shape=jax.ShapeDtypeStruct((bsz * t, B_HEADS * LANES), BF16),
        scratch_shapes=[pltpu.VMEM((t // tk, tq, tk), F32),
                        pltpu.VMEM((B_HEADS, tq, 1), F32), pltpu.VMEM((B_HEADS, tq, 1), F32),
                        pltpu.VMEM((B_HEADS, tq, LANES), F32)],
        compiler_params=_cparams("arbitrary", "arbitrary"),
        name="dsa_prompt",
    )(qb, qi, wi, ki, kb16, vb16)


def _page_specs(block, n_rep):
    def spec(r):
        return pl.BlockSpec(block, lambda b, pg, pt: (pt[b, pg * n_rep + r],) + (0,) * (len(block) - 1))
    return [spec(r) for r in range(n_rep)]


def _spread_matrix(n_kv, lower_tri):
    r = lax.broadcasted_iota(I32, (PAGE_SIZE, PAGE_SIZE * n_kv), 0)
    key = lax.shift_right_logical(lax.broadcasted_iota(I32, (PAGE_SIZE, PAGE_SIZE * n_kv), 1),
                                  n_kv.bit_length() - 1)
    return jnp.where((r <= key) if lower_tri else (r == key), 1.0, 0.0)


def _fscan_kernel(pt_ref, *refs, n_rep, n_kv):
    pages = refs[:n_rep]
    new_ref, fp_ref, fn_ref, carry_sc = refs[n_rep:]
    pg = pl.program_id(1)
    h = new_ref.shape[0]
    wide = PAGE_SIZE * n_kv
    m = n_rep * h
    hp = lax.Precision.HIGHEST

    @pl.when(pg == 0)
    def _():
        carry_sc[...] = jnp.zeros(carry_sc.shape, F32)

    x = jnp.concatenate([pages[rr][...] for rr in range(n_rep)], axis=0)
    cs = jnp.dot(x, _spread_matrix(n_kv, True), precision=hp, preferred_element_type=F32)
    tot = cs[:, wide - 1:wide]
    i = lax.broadcasted_iota(I32, (m, m), 0)
    j = lax.broadcasted_iota(I32, (m, m), 1)
    sh = h.bit_length() - 1
    earlier_page = jnp.where(lax.shift_right_logical(j, sh) < lax.shift_right_logical(i, sh), 1.0, 0.0)
    lower = jnp.where((i & (h - 1)) == (j & (h - 1)), earlier_page, 0.0)
    pref = jnp.dot(lower, jnp.broadcast_to(tot, (m, LANES)), precision=hp, preferred_element_type=F32)[:, :1]
    carry = carry_sc[...]
    out = cs + (pref + jnp.concatenate([carry] * n_rep, axis=0))
    for rr in range(n_rep):
        fp_ref[:, rr * wide:(rr + 1) * wide] = -out[rr * h:(rr + 1) * h, :]
    carry = carry + pref[m - h:, :] + tot[m - h:, :]
    carry_sc[...] = carry

    @pl.when(pg == pl.num_programs(1) - 1)
    def _():
        r = lax.broadcasted_iota(I32, (PAGE_SIZE, PAGE_SIZE), 0)
        c = lax.broadcasted_iota(I32, (PAGE_SIZE, PAGE_SIZE), 1)
        tri = jnp.where(r <= c, 1.0, 0.0)
        fn_ref[...] = jnp.dot(new_ref[...], tri, precision=hp, preferred_element_type=F32) + carry


def _fscan(page_table, logf_pool_t, logf_new_t, n_rep, n_kv):
    bsz, n_pages = page_table.shape
    h = logf_pool_t.shape[1]
    wide = PAGE_SIZE * n_kv
    grid_spec = pltpu.PrefetchScalarGridSpec(
        num_scalar_prefetch=1, grid=(bsz, n_pages // n_rep),
        in_specs=_page_specs((None, h, PAGE_SIZE), n_rep)
        + [pl.BlockSpec((None, h, PAGE_SIZE), lambda b, pg, pt: (b, 0, 0))],
        out_specs=[pl.BlockSpec((None, h, n_rep * wide), lambda b, pg, pt: (b, 0, pg)),
                   pl.BlockSpec((None, h, PAGE_SIZE), lambda b, pg, pt: (b, 0, 0))],
        scratch_shapes=[pltpu.VMEM((h, 1), F32)])
    return pl.pallas_call(
        functools.partial(_fscan_kernel, n_rep=n_rep, n_kv=n_kv), grid_spec=grid_spec,
        out_shape=[jax.ShapeDtypeStruct((bsz, h, n_pages * wide), F32),
                   jax.ShapeDtypeStruct((bsz, h, PAGE_SIZE), F32)],
        compiler_params=_cparams("arbitrary", "arbitrary"), name="fscan",
    )(page_table, *([logf_pool_t] * n_rep), logf_new_t)


def _sidx_kernel(pt_ref, *refs, n_rep, n_tok, topk, n_kv):
    pages = refs[:n_rep]
    qi_ref, wi_ref, kn_ref, kb_ref, nb_ref, sc_sc = refs[n_rep:]
    pg = pl.program_id(1)
    npg = pl.num_programs(1)
    rows = 8
    pad = jnp.zeros((rows - n_tok, PAGE_SIZE), F32)

    def score(keys_f32):
        d = jnp.maximum(_dot_nt(qi_ref[...], keys_f32.astype(BF16)), 0.0) * wi_ref[...]
        return jnp.concatenate([jnp.sum(d.reshape(n_tok, IDX_HEADS, PAGE_SIZE), axis=1), pad], axis=0)

    sc_sc[pg] = jnp.concatenate([score(pages[rr][...]) for rr in range(n_rep)], axis=1)

    @pl.when(pg == npg - 1)
    def _():
        t_idx = lax.broadcasted_iota(I32, (rows, PAGE_SIZE), 0)
        s_idx = lax.broadcasted_iota(I32, (rows, PAGE_SIZE), 1)
        new_valid = (s_idx <= t_idx) & (t_idx < n_tok)
        new_raw = score(kn_ref[...])
        new = jnp.where(new_valid, new_raw, -jnp.inf)
        past = sc_sc[...]

        def count_ge(cand):
            c_past = jnp.sum(jnp.sum(jnp.where(past >= cand[None], 1.0, 0.0), axis=0), axis=-1, keepdims=True)
            return c_past + jnp.sum(jnp.where(new >= cand, 1.0, 0.0), axis=-1, keepdims=True)

        mx = jnp.maximum(jnp.max(jnp.max(past, axis=0), axis=-1, keepdims=True),
                         jnp.max(new, axis=-1, keepdims=True))
        mn = jnp.minimum(jnp.min(jnp.min(past, axis=0), axis=-1, keepdims=True),
                         jnp.min(jnp.where(new_valid, new_raw, jnp.inf), axis=-1, keepdims=True))
        tok = lax.broadcasted_iota(I32, (rows, 1), 0)
        n_valid = jnp.where(tok < n_tok, (past.shape[0] * past.shape[2] + 1 + tok).astype(F32), 0.0)
        many = n_valid > topk
        thr = _kth_value(count_ge, jnp.where(many, mn, LOWEST), jnp.where(many, mx, LOWEST),
                         jnp.where(many, n_valid, 0.0), float(topk))
        spread = _spread_matrix(n_kv, False).astype(BF16)
        wide = PAGE_SIZE * n_kv
        for s in range(past.shape[0]):
            sel = jnp.where(past[s] >= thr, 1.0, 0.0)
            stacked = jnp.concatenate([sel[:, c * PAGE_SIZE:(c + 1) * PAGE_SIZE] for c in range(n_rep)], axis=0)
            ex = _dot(stacked.astype(BF16), spread)
            for c in range(n_rep):
                kb_ref[:, (s * n_rep + c) * wide:(s * n_rep + c + 1) * wide] = jnp.where(
                    ex[c * rows:(c + 1) * rows, :] > 0.5, 0.0, NEG)
        nb_ref[...] = jnp.where(new >= thr, 0.0, NEG)


def _sample_index(page_table, idx_pool, qi, wi, ki_new, n_rep, n_tok, n_kv):
    bsz, n_pages = page_table.shape
    p = n_pages * PAGE_SIZE
    topk = min(TOPK_MAX, (p + n_tok) // 4)
    nr = n_tok * IDX_HEADS
    steps = n_pages // n_rep
    wide = n_rep * PAGE_SIZE
    grid_spec = pltpu.PrefetchScalarGridSpec(
        num_scalar_prefetch=1, grid=(bsz, steps),
        in_specs=_page_specs((None, PAGE_SIZE, IDX_DIM), n_rep)
        + [pl.BlockSpec((None, nr, IDX_DIM), lambda b, pg, pt: (b, 0, 0)),
           pl.BlockSpec((None, nr, 1), lambda b, pg, pt: (b, 0, 0)),
           pl.BlockSpec((None, PAGE_SIZE, IDX_DIM), lambda b, pg, pt: (b, 0, 0))],
        out_specs=[pl.BlockSpec((None, 8, p * n_kv), lambda b, pg, pt: (b, 0, 0)),
                   pl.BlockSpec((None, 8, PAGE_SIZE), lambda b, pg, pt: (b, 0, 0))],
        scratch_shapes=[pltpu.VMEM((steps, 8, wide), F32)])
    return pl.pallas_call(
        functools.partial(_sidx_kernel, n_rep=n_rep, n_tok=n_tok, topk=topk, n_kv=n_kv), grid_spec=grid_spec,
        out_shape=[jax.ShapeDtypeStruct((bsz, 8, p * n_kv), F32),
                   jax.ShapeDtypeStruct((bsz, 8, PAGE_SIZE), F32)],
        compiler_params=_cparams("arbitrary", "arbitrary"), name="sample_index",
    )(page_table, *([idx_pool] * n_rep), qi, wi, ki_new)


def _paged_attn_kernel(pt_ref, *refs, n_rep, n_kv, n_heads, per_head):
    kp = refs[:n_rep]
    vp = refs[n_rep:2 * n_rep]
    q_ref, kb_ref, rb_ref, kn_ref, vn_ref, nb_ref, o_ref, m_sc, l_sc, acc_sc = refs[2 * n_rep:]
    pg = pl.program_id(1)
    nrow = q_ref.shape[0]
    wide = PAGE_SIZE * n_kv

    @pl.when(pg == 0)
    def _():
        m_sc[...] = jnp.full(m_sc.shape, NEG, F32)
        l_sc[...] = jnp.zeros(l_sc.shape, F32)
        acc_sc[...] = jnp.zeros(acc_sc.shape, F32)

    def expand(kb):
        if per_head:
            return jnp.concatenate([kb] * (nrow // kb.shape[0]), axis=0)
        return jnp.concatenate([jnp.broadcast_to(kb[t:t + 1], (8, kb.shape[1])) for t in range(nrow // 8)], axis=0)

    head = lax.broadcasted_iota(I32, (nrow, wide), 0) % n_heads
    kv_col = lax.broadcasted_iota(I32, (nrow, wide), 1) % n_kv
    head_bias = jnp.where(head // (n_heads // n_kv) == kv_col, 0.0, NEG)
    q = q_ref[...]
    rb = rb_ref[...]

    def update(s, vs, width):
        m_prev = m_sc[...]
        m_new = jnp.maximum(m_prev, jnp.max(s, axis=-1, keepdims=True))
        alpha = jnp.exp(m_prev - m_new)
        p = jnp.exp(s - m_new)
        l_sc[...] = alpha * l_sc[...] + jnp.sum(p, axis=-1, keepdims=True)
        acc = alpha * acc_sc[...]
        for r, v in enumerate(vs):
            acc = acc + _dot(p[:, r * width:(r + 1) * width].astype(BF16), v.astype(BF16))
        acc_sc[...] = acc
        m_sc[...] = m_new

    s = jnp.concatenate([_dot_nt(q, kp[r][...].astype(BF16)) for r in range(n_rep)], axis=1)
    bias = expand(kb_ref[...]) + jnp.concatenate([head_bias] * n_rep, axis=1)
    update(s + rb + bias, [vp[r][...] for r in range(n_rep)], wide)

    @pl.when(pg == pl.num_programs(1) - 1)
    def _():
        sn = _dot_nt(q, kn_ref[...].astype(BF16)) + rb + nb_ref[...]
        update(sn, [vn_ref[...]], PAGE_SIZE)
        o_ref[...] = acc_sc[...] / l_sc[...]


def _paged_attn(page_table, k_pool, v_pool, q, key_bias, row_bias, k_new, v_new, new_bias, n_rep, n_kv, n_heads,
                per_head):
    bsz, n_pages = page_table.shape
    nrow, w = q.shape[1], q.shape[2]
    kbh = key_bias.shape[1]
    wide = PAGE_SIZE * n_kv
    const = lambda b, pg, pt: (b, 0, 0)
    grid_spec = pltpu.PrefetchScalarGridSpec(
        num_scalar_prefetch=1, grid=(bsz, n_pages // n_rep),
        in_specs=_page_specs((wide, w), n_rep) + _page_specs((wide, w), n_rep)
        + [pl.BlockSpec((None, nrow, w), const),
           pl.BlockSpec((None, kbh, n_rep * wide), lambda b, pg, pt: (b, 0, pg)),
           pl.BlockSpec((None, nrow, 1), const),
           pl.BlockSpec((None, PAGE_SIZE, w), const),
           pl.BlockSpec((None, PAGE_SIZE, w), const),
           pl.BlockSpec((None, nrow, PAGE_SIZE), const)],
        out_specs=pl.BlockSpec((None, nrow, w), const),
        scratch_shapes=[pltpu.VMEM((nrow, 1), F32), pltpu.VMEM((nrow, 1), F32), pltpu.VMEM((nrow, w), F32)])
    return pl.pallas_call(
        functools.partial(_paged_attn_kernel, n_rep=n_rep, n_kv=n_kv, n_heads=n_heads, per_head=per_head),
        grid_spec=grid_spec,
        out_shape=jax.ShapeDtypeStruct((bsz, nrow, w), F32),
        compiler_params=_cparams("arbitrary", "arbitrary"), name="paged_attn",
    )(page_table, *([k_pool] * n_rep), *([v_pool] * n_rep), q, key_bias, row_bias, k_new, v_new, new_bias)


def _merge_kernel(oa_ref, ob_ref, ga_ref, gb_ref, wa_ref, wb_ref, o_ref):
    m = ga_ref[...] * _dot(oa_ref[...], wa_ref[...]) + gb_ref[...] * _dot(ob_ref[...], wb_ref[...])
    o_ref[...] = m.astype(o_ref.dtype)


def _merge(oa, ob, ga, gb, wa, wb, tm):
    n, e = oa.shape
    d = wa.shape[1]
    row = lambda i: (i, 0)
    const = lambda i: (0, 0)
    return pl.pallas_call(
        _merge_kernel, grid=(n // tm,),
        in_specs=[pl.BlockSpec((tm, e), row), pl.BlockSpec((tm, e), row),
                  pl.BlockSpec((tm, d), row), pl.BlockSpec((tm, d), row),
                  pl.BlockSpec((e, d), const), pl.BlockSpec((e, d), const)],
        out_specs=pl.BlockSpec((tm, d), row),
        out_shape=jax.ShapeDtypeStruct((n, d), BF16),
        compiler_params=_cparams("arbitrary"), name="merge",
    )(oa, ob, ga, gb, wa, wb)


def _outproj_kernel(x_ref, m_ref, w_ref, g1_ref, gn_ref, sc_ref, sh_ref, x1_ref, h2t_ref):
    x1 = x_ref[...] + g1_ref[...] * _dot(m_ref[...], w_ref[...])
    x1_ref[...] = x1
    h2t_ref[...] = _rms_mod(x1, gn_ref[...], sc_ref[...], sh_ref[...]).T.astype(h2t_ref.dtype)


def _outproj(x, merged, w_out, ga1, g_norm2, sc2, sh2, tm, rows_per_group):
    n, d = x.shape
    row = lambda i: (i, 0)
    return pl.pallas_call(
        _outproj_kernel, grid=(n // tm,),
        in_specs=[pl.BlockSpec((tm, d), row), pl.BlockSpec((tm, d), row),
                  pl.BlockSpec((d, d), lambda i: (0, 0)),
                  _mod_spec(ga1, tm, rows_per_group),
                  pl.BlockSpec((1, d), lambda i: (0, 0)),
                  _mod_spec(sc2, tm, rows_per_group), _mod_spec(sh2, tm, rows_per_group)],
        out_specs=[pl.BlockSpec((tm, d), row), pl.BlockSpec((d, tm), lambda i: (0, i))],
        out_shape=[jax.ShapeDtypeStruct((n, d), F32), jax.ShapeDtypeStruct((d, n), BF16)],
        compiler_params=_cparams("arbitrary"), name="outproj",
    )(x, merged, w_out, ga1, g_norm2.reshape(1, d), sc2, sh2)


def _top_values(x, n):
    out = []
    for _ in range(n):
        m = jnp.max(x, axis=0, keepdims=True)
        out.append(m)
        x = jnp.where(x == m, -jnp.inf, x)
    return out


def _peer_query_kernel(ht_ref, wq_ref, sk_ref, s_ref, thr_ref, m1_ref, m2_ref, rz_ref, sv_sc):
    qt = _dot(wq_ref[...], ht_ref[...])
    n_hc = sk_ref.shape[0]
    for hc in range(n_hc):
        s = _dot(sk_ref[hc], qt[hc * PEER_KEYS:(hc + 1) * PEER_KEYS, :].astype(BF16))
        s_ref[hc] = s
        for r, m in enumerate(_top_values(s, PEER_TOPK)):
            sv_sc[hc, r:r + 1, :] = m
    for h in range(n_hc // 2):
        sv1 = sv_sc[2 * h]
        sv2 = sv_sc[2 * h + 1]
        cands = [sv1[0:1] + sv2]
        cands += [sv1[a:a + 1] + sv2[0:8] for a in range(1, 8)]
        cands += [sv1[8:16] + sv2[0:1]]
        tops = _top_values(jnp.concatenate(cands, axis=0), PEER_TOPK)
        z = jnp.zeros_like(tops[0])
        for m in tops:
            z = z + jnp.exp(m - tops[0])
        thr_ref[h:h + 1, :] = tops[-1]
        m1_ref[h:h + 1, :] = sv1[0:1]
        m2_ref[h:h + 1, :] = sv2[0:1]
        rz_ref[h:h + 1, :] = 1.0 / z


def _peer_query(h2t, wqt, subkeys, tm):
    d, n = h2t.shape
    n_hc = subkeys.shape[0]
    col = lambda i: (0, i)
    small = jax.ShapeDtypeStruct((PEER_HEADS, n), F32)
    small_spec = pl.BlockSpec((PEER_HEADS, tm), col)
    return pl.pallas_call(
        _peer_query_kernel, grid=(n // tm,),
        in_specs=[pl.BlockSpec((d, tm), col), pl.BlockSpec(wqt.shape, lambda i: (0, 0)),
                  pl.BlockSpec(subkeys.shape, lambda i: (0, 0, 0))],
        out_specs=[pl.BlockSpec((n_hc, PEER_KEYS, tm), lambda i: (0, 0, i))] + [small_spec] * 4,
        out_shape=[jax.ShapeDtypeStruct((n_hc, PEER_KEYS, n), F32)] + [small] * 4,
        scratch_shapes=[pltpu.VMEM((n_hc, PEER_TOPK, tm), F32)],
        compiler_params=_cparams("arbitrary"), name="peer_query",
    )(h2t, wqt, subkeys)


def _peer_weighted_acts(tile, live, a_ref, wa_ref, s_ref, thr_ref, m1_ref, e2_sc):
    ts, tm = a_ref.shape
    half = PEER_KEYS // 2

    def block(q, l):
        def run(after):
            i1 = jnp.clip(tile * (ts // PEER_KEYS) + q, 0, PEER_KEYS - 1)
            lanes = slice(l * LANES, (l + 1) * LANES)
            s1 = [s_ref[2 * h, pl.ds(i1, 1), :][:, lanes] for h in range(PEER_HEADS)]
            e1 = [jnp.exp(s1[h] - m1_ref[h:h + 1, lanes]) for h in range(PEER_HEADS)]
            thr = [jnp.where(live, thr_ref[h:h + 1, lanes], jnp.inf) + after for h in range(PEER_HEADS)]
            halves = []
            for r in range(2):
                rows = slice(r * half, (r + 1) * half)
                w = jnp.zeros((half, LANES), F32)
                for h in range(PEER_HEADS):
                    sel = (s_ref[2 * h + 1, rows, lanes] + s1[h]) >= thr[h]
                    w = w + jnp.where(sel, e2_sc[h, rows, lanes] * e1[h], 0.0)
                a = a_ref[q * PEER_KEYS + r * half:q * PEER_KEYS + (r + 1) * half, lanes]
                halves.append(w * (0.5 * a * (1.0 + lax.erf(a * (2.0 ** -0.5)))))
            blk = jnp.concatenate(halves, axis=0)
            wa_ref[lanes, q * PEER_KEYS:(q + 1) * PEER_KEYS] = blk.T.astype(wa_ref.dtype)
        return run

    return [block(q, l) for q in range(ts // PEER_KEYS) for l in range(tm // LANES)]


def _interleave(vpu_tasks, mxu_tasks):
    n_v, n_m = len(vpu_tasks), len(mxu_tasks)
    done_m = 0
    after = jnp.zeros((1, LANES), F32)
    for i, task in enumerate(vpu_tasks):
        while done_m * n_v < (i + 1) * n_m:
            row = mxu_tasks[done_m]()
            after = jnp.where(row != row, row, 0.0)
            done_m += 1
        task(after)


def _peer_dense_kernel(ht_ref, u_ref, v_ref, s_ref, thr_ref, m1_ref, m2_ref, rz_ref, x1_ref, g2_ref, o_ref,
                       a0, a1, wa0, wa1, e2_sc):
    k = pl.program_id(1)
    last = pl.num_programs(1) - 1
    ts = a0.shape[0]

    @pl.when(k == 0)
    def _():
        o_ref[...] = jnp.zeros(o_ref.shape, F32)
        wa0[...] = jnp.zeros(wa0.shape, wa0.dtype)
        a1[...] = jnp.zeros(a1.shape, F32)
        for h in range(PEER_HEADS):
            e2_sc[h] = jnp.exp(s_ref[2 * h + 1] - m2_ref[h:h + 1, :]) * rz_ref[h:h + 1, :]

    d = o_ref.shape[1]
    cw = min(d, MXU_COLS)
    rh = ts // 2

    def act_chunk(a_ref, base, c):
        def run():
            r = _dot(u_ref[base + c * rh:base + (c + 1) * rh, :], ht_ref[...])
            a_ref[c * rh:(c + 1) * rh, :] = r
            return r[rh - 1:rh, :LANES]
        return run

    def val_chunk(wa_ref, base, c):
        def run():
            r = _dot(wa_ref[...], v_ref[base:base + ts, c * cw:(c + 1) * cw])
            o_ref[:, c * cw:(c + 1) * cw] += r
            return r[r.shape[0] - 1:, :LANES]
        return run

    weights = lambda tile, live, a_ref, wa_ref: _peer_weighted_acts(tile, live, a_ref, wa_ref, s_ref, thr_ref,
                                                                    m1_ref, e2_sc)
    _interleave(weights(2 * k - 1, k >= 1, a1, wa1),
                [act_chunk(a0, 0, c) for c in range(2)] + [val_chunk(wa0, 0, c) for c in range(d // cw)])
    _interleave(weights(2 * k, k < last, a0, wa0),
                [act_chunk(a1, ts, c) for c in range(2)] + [val_chunk(wa1, ts, c) for c in range(d // cw)])

    @pl.when(k == last)
    def _():
        o_ref[...] = x1_ref[...] + g2_ref[...] * o_ref[...]


def _peer_dense(h2t, u16, v16, s_t, thr, m1, m2, rz, x1, ga2, tm, ts, rows_per_group):
    d, n = h2t.shape
    n_exp = u16.shape[0]
    n_hc = s_t.shape[0]
    n_k = n_exp // (2 * ts)
    tok = lambda i, k: (0, i)
    small_spec = pl.BlockSpec((PEER_HEADS, tm), tok)
    g2_spec = pl.BlockSpec((None,) + ga2.shape[1:], lambda i, k: ((i * tm) // rows_per_group, 0, 0))
    return pl.pallas_call(
        _peer_dense_kernel,
        grid=(n // tm, n_k + 1),
        in_specs=[pl.BlockSpec((d, tm), tok),
                  pl.BlockSpec((2 * ts, d), lambda i, k: (jnp.minimum(k, n_k - 1), 0)),
                  pl.BlockSpec((2 * ts, d), lambda i, k: (jnp.maximum(k - 1, 0), 0)),
                  pl.BlockSpec((n_hc, PEER_KEYS, tm), lambda i, k: (0, 0, i))] + [small_spec] * 4
        + [pl.BlockSpec((tm, d), lambda i, k: (i, 0)), g2_spec],
        out_specs=pl.BlockSpec((tm, d), lambda i, k: (i, 0)),
        out_shape=jax.ShapeDtypeStruct((n, d), F32),
        scratch_shapes=[pltpu.VMEM((ts, tm), F32), pltpu.VMEM((ts, tm), F32),
                        pltpu.VMEM((tm, ts), BF16), pltpu.VMEM((tm, ts), BF16),
                        pltpu.VMEM((PEER_HEADS, PEER_KEYS, tm), F32)],
        compiler_params=_cparams("arbitrary", "arbitrary"), name="peer_dense",
    )(h2t, u16, v16, s_t, thr, m1, m2, rz, x1, ga2)


def _split_w_in(w_in, d_model):
    sizes = (A_HEADS * HEAD_DIM, A_KV_HEADS * HEAD_DIM, A_KV_HEADS * HEAD_DIM, A_HEADS,
             B_HEADS * HEAD_DIM, B_KV_HEADS * HEAD_DIM, B_KV_HEADS * HEAD_DIM,
             IDX_HEADS * IDX_DIM, IDX_DIM, IDX_HEADS, d_model, d_model)
    offs = [0]
    for s in sizes:
        offs.append(offs[-1] + s)
    cols = lambda a, b: w_in[:, offs[a]:offs[b]].astype(BF16)
    padw = lambda a: jnp.pad(w_in[:, offs[a]:offs[a + 1]], ((0, 0), (0, LANES - sizes[a]))).astype(BF16)
    return dict(a=cols(0, 3), b=cols(4, 7), qi=cols(7, 8),
                misc=jnp.concatenate([padw(3), padw(8), padw(9)], axis=1),
                gate_a=cols(10, 11), gate_b=cols(11, 12))


def _front(x2, mods, pos, w, p, tm, rows_per_group):
    sh1, sc1 = mods[0], mods[1]
    h = _prenorm(x2, p["g_norm1"], sc1, sh1, tm, rows_per_group)
    tab128, half128 = _rope_tables(pos, HEAD_DIM)
    tab64, half64 = _rope_tables(pos, IDX_DIM)
    gain_a = jnp.concatenate([jnp.tile(p["g_qn_a"], A_HEADS), jnp.tile(p["g_kn_a"], A_KV_HEADS)]).reshape(1, -1)
    gain_b = jnp.concatenate([jnp.tile(p["g_qn_b"], B_HEADS), jnp.tile(p["g_kn_b"], B_KV_HEADS)]).reshape(1, -1)
    q_a, k_a, v_a, ka16, va16 = _qkv_proj(h, w["a"], gain_a, None, 0, tm, A_HEADS, A_KV_HEADS)
    q_b, k_b, v_b, kb16, vb16 = _qkv_proj(h, w["b"], gain_b, tab128, half128, tm, B_HEADS, B_KV_HEADS)
    q_i = _qidx_proj(h, w["qi"], tab64, half64, tm)
    logf, k_i, w_i = _misc_proj(h, w["misc"], p["b_fgate"], tab64, half64, tm)
    gate_a = _gate_proj(h, w["gate_a"], tm)
    gate_b = _gate_proj(h, w["gate_b"], tm)
    return dict(q_a=q_a, k_a=k_a, v_a=v_a, ka16=ka16, va16=va16, logf=logf,
                q_b=q_b, k_b=k_b, v_b=v_b, kb16=kb16, vb16=vb16,
                q_i=q_i, k_i=k_i, w_i=w_i, gate_a=gate_a, gate_b=gate_b)


def _back(x2, f, o_a, o_b, mods, p, tm, rows_per_group, tm_peer, te):
    ga1, sh2, sc2, ga2 = mods[2], mods[3], mods[4], mods[5]
    merged = _merge(o_a, o_b, f["gate_a"], f["gate_b"], p["w_branch_a"], p["w_branch_b"], tm)
    x1, h2t = _outproj(x2, merged, p["w_out"], ga1, p["g_norm2"], sc2, sh2, tm, rows_per_group)
    s_t, thr, m1, m2, rz = _peer_query(h2t, p["wq_t"], p["subkeys"], tm_peer)
    return _peer_dense(h2t, p["u16"], p["v16"], s_t, thr, m1, m2, rz, x1, ga2, tm_peer, te, rows_per_group)


def _prompt_group(x, ada, w, p):
    bsz, t, d = x.shape
    tm = _tile(t, 512)
    mods = [ada[:, i].reshape(bsz, 1, d) for i in range(N_ADA)]
    x2 = x.reshape(bsz * t, d)
    f = _front(x2, mods, jnp.arange(t), w, p, tm, t)
    fk = _cumsum_lanes(f["logf"].reshape(bsz, t, A_HEADS).transpose(0, 2, 1))
    o_a = _fox_prompt(f["q_a"], f["ka16"], f["va16"], fk.transpose(0, 2, 1), fk, bsz, t)
    o_b = _dsa_prompt(f["q_b"], f["q_i"], f["w_i"], f["k_i"], f["kb16"], f["vb16"], bsz, t)
    y = _back(x2, f, o_a, o_b, mods, p, tm, t, _tile(bsz * t, 512), 512)
    return y.reshape(bsz, t, d), f


def _new_key_bias(base, n_heads, n_kv):
    bsz, n_tok = base.shape[0], base.shape[1]
    head_ok = (jnp.arange(n_heads)[:, None] // (n_heads // n_kv)) == jnp.arange(n_kv)[None, :]
    full = jnp.where(head_ok[None, None, :, None, :], base[..., None], NEG)
    full = full.reshape(bsz, n_tok * n_heads, n_tok * n_kv)
    return jnp.pad(full, ((0, 0), (0, 0), (0, PAGE_SIZE - n_tok * n_kv)), constant_values=NEG)


def _pad_rows(a, rows):
    return jnp.pad(a, ((0, 0), (0, rows - a.shape[1]), (0, 0)))


def _sample_group(x, ada, w, p, caches, page_table):
    bsz, n_tok, d = x.shape
    n = bsz * n_tok
    ck_a, cv_a, clf, ck_b, cv_b, cki = caches
    n_pages = page_table.shape[1]
    past = n_pages * PAGE_SIZE
    tm = _tile(n, 512)
    mods = [jnp.repeat(ada[:, i], n_tok, axis=0).reshape(n // tm, tm, d) for i in range(N_ADA)]
    x2 = x.reshape(n, d)
    pos = jnp.tile(past + jnp.arange(n_tok), bsz)
    f = _front(x2, mods, pos, w, p, tm, tm)
    rep_attn, rep_small = _rep(n_pages, 16), _rep(n_pages, 32)
    t_idx = jnp.arange(n_tok)
    causal = t_idx[None, :] <= t_idx[:, None]

    def rows128(a, n_kv):
        return _pad_rows(a.reshape(bsz, n_tok * n_kv, HEAD_DIM), PAGE_SIZE)

    lf_new = jnp.pad(f["logf"].reshape(bsz, n_tok, A_HEADS).transpose(0, 2, 1),
                     ((0, 0), (0, 0), (0, PAGE_SIZE - n_tok)))
    neg_fp, fn_t = _fscan(page_table, clf, lf_new, rep_small, A_KV_HEADS)
    fn = fn_t[:, :, :n_tok].transpose(0, 2, 1)
    base = jnp.where(causal[None, :, None, :], -fn.transpose(0, 2, 1)[:, None, :, :], NEG)
    o_a = _paged_attn(page_table, ck_a, cv_a, f["q_a"].reshape(bsz, n_tok * A_HEADS, HEAD_DIM),
                      neg_fp, fn.reshape(bsz, n_tok * A_HEADS, 1),
                      rows128(f["k_a"], A_KV_HEADS), rows128(f["v_a"], A_KV_HEADS),
                      _new_key_bias(base, A_HEADS, A_KV_HEADS), rep_attn, A_KV_HEADS, A_HEADS, True)
    o_a = o_a.reshape(n, A_HEADS * HEAD_DIM).astype(BF16)

    qi = f["q_i"].reshape(IDX_HEADS, bsz, n_tok, IDX_DIM).transpose(1, 2, 0, 3).reshape(bsz, n_tok * IDX_HEADS, IDX_DIM)
    wi = f["w_i"].reshape(bsz, n_tok * IDX_HEADS, 1)
    key_bias, nbm = _sample_index(page_table, cki, qi, wi,
                                  _pad_rows(f["k_i"].reshape(bsz, n_tok, IDX_DIM), PAGE_SIZE),
                                  rep_small, n_tok, B_KV_HEADS)
    base = jnp.broadcast_to(nbm[:, :n_tok, None, :n_tok], (bsz, n_tok, B_HEADS, n_tok))
    o_b = _paged_attn(page_table, ck_b, cv_b, f["q_b"].reshape(bsz, n_tok * B_HEADS, HEAD_DIM),
                      key_bias, jnp.zeros((bsz, n_tok * B_HEADS, 1), F32),
                      rows128(f["k_b"], B_KV_HEADS), rows128(f["v_b"], B_KV_HEADS),
                      _new_key_bias(base, B_HEADS, B_KV_HEADS), rep_attn, B_KV_HEADS, B_HEADS, False)
    o_b = o_b.reshape(n, B_HEADS * HEAD_DIM).astype(BF16)

    y = _back(x2, f, o_a, o_b, mods, p, tm, tm, _tile(n, 512), 512)
    return y.reshape(bsz, n_tok, d), f


def _layer_outputs(f, bsz, t):
    return (f["k_a"].reshape(bsz, t, A_KV_HEADS, HEAD_DIM), f["v_a"].reshape(bsz, t, A_KV_HEADS, HEAD_DIM),
            f["logf"].reshape(bsz, t, A_HEADS),
            f["k_b"].reshape(bsz, t, B_KV_HEADS, HEAD_DIM), f["v_b"].reshape(bsz, t, B_KV_HEADS, HEAD_DIM),
            f["k_i"].reshape(bsz, t, IDX_DIM))


def kernel(x_prompt, x_sample, cache_fox_k, cache_fox_v, cache_fox_logf, cache_dsa_k, cache_dsa_v, cache_idx_k, page_table, c_prompt, c_sample, w_ada, b_ada, g_norm1, g_norm2, w_in, b_fgate, g_qn_a, g_kn_a, g_qn_b, g_kn_b, w_branch_a, w_branch_b, w_out, w_peer_q, peer_subkeys, peer_u, peer_v):
    depth = w_ada.shape[0]
    bsz, t, d = x_prompt.shape
    bs, ts, _ = x_sample.shape
    xp, xs = x_prompt, x_sample
    rows_p, rows_s = [], []
    n_c = bsz + bs
    c_all = jnp.pad(jnp.concatenate([c_prompt, c_sample], axis=0), ((0, (-n_c) % 8), (0, 0)))
    n_pool = cache_fox_k.shape[1]
    pools = (cache_fox_k.reshape(-1, HEAD_DIM), cache_fox_v.reshape(-1, HEAD_DIM),
             cache_fox_logf.transpose(0, 1, 3, 2).reshape(depth * n_pool, A_HEADS, PAGE_SIZE),
             cache_dsa_k.reshape(-1, HEAD_DIM), cache_dsa_v.reshape(-1, HEAD_DIM),
             cache_idx_k.reshape(depth * n_pool, PAGE_SIZE, IDX_DIM))
    for l in range(depth):
        ada = _ada(c_all, w_ada[l], b_ada[l]).reshape(c_all.shape[0], N_ADA, d)
        w = _split_w_in(w_in[l], d)
        p = dict(g_norm1=g_norm1[l], g_norm2=g_norm2[l], b_fgate=b_fgate[l], g_qn_a=g_qn_a[l], g_kn_a=g_kn_a[l],
                 g_qn_b=g_qn_b[l], g_kn_b=g_kn_b[l],
                 w_branch_a=w_branch_a[l].astype(BF16), w_branch_b=w_branch_b[l].astype(BF16),
                 w_out=w_out[l].astype(BF16), wq_t=w_peer_q[l].T.astype(BF16),
                 subkeys=peer_subkeys[l].reshape(PEER_HEADS * 2, PEER_KEYS, -1).astype(BF16),
                 u16=peer_u[l].astype(BF16), v16=peer_v[l].astype(BF16))
        xp, f_p = _prompt_group(xp, ada[:bsz], w, p)
        xs, f_s = _sample_group(xs, ada[bsz:n_c], w, p, pools, page_table + l * n_pool)
        rows_p.append(_layer_outputs(f_p, bsz, t))
        rows_s.append(_layer_outputs(f_s, bs, ts))
    outs_p = [jnp.stack([r[i] for r in rows_p], axis=0) for i in range(6)]
    outs_s = [jnp.stack([r[i] for r in rows_s], axis=0) for i in range(6)]
    return (xp, xs, *outs_p, *outs_s)
```
